```python
import math
import jax, jax.numpy as jnp
from jax import lax
import numpy as np

D_MODEL = 1024
BATCH = 8
SEQ = 4096
DEPTH = 2

DN_HEADS = 8
DN_HEAD_DIM = 128
DN_WIDTH = DN_HEADS * DN_HEAD_DIM
DN_CONV = 4
DN_CHUNK = 64
DA_HEADS = 12
DA_HEAD_DIM = 64
DA_WIDTH = DA_HEADS * DA_HEAD_DIM
DA_PATTERNS = ((128, 1), (512, 4), (2048, 16))
DA_BLOCK = 128
ALIBI_MAX_EXP = 8.0
D_FF = 2816
MACARON_WEIGHT = 0.5
NORM_EPS = 1e-6
N_ADA = 9
IN_SPLITS = (3 * DN_WIDTH, DN_WIDTH, DN_HEADS, DN_HEADS,
             DA_WIDTH, DA_WIDTH, DA_WIDTH, D_MODEL, D_MODEL)
IN_COLS = 3 * DN_WIDTH + DN_WIDTH + 2 * DN_HEADS + 3 * DA_WIDTH + 2 * D_MODEL

kernel_name = "hybrid_deltanet_dilated_attn_macaron_adaln"


def rmsnorm(x, g):
    xf = x.astype(jnp.float32)
    y = xf * lax.rsqrt(jnp.mean(xf * xf, axis=-1, keepdims=True) + NORM_EPS)
    return (y * g.astype(jnp.float32)).astype(x.dtype)


def l2norm(x):
    xf = x.astype(jnp.float32)
    return xf * lax.rsqrt(jnp.sum(xf * xf, axis=-1, keepdims=True) + NORM_EPS)


def modulate(x, shift, scale):
    return x * (1.0 + scale[:, None, :]) + shift[:, None, :]


def swiglu(x, w_gate, w_up, w_down):
    return (jax.nn.silu(x @ w_gate) * (x @ w_up)) @ w_down


def causal_depthwise_conv(x, w):
    K = w.shape[0]
    S = x.shape[1]
    xp = jnp.pad(x, ((0, 0), (K - 1, 0), (0, 0)))
    y = xp[:, 0:S] * w[0]
    for j in range(1, K):
        y = y + xp[:, j:j + S] * w[j]
    return y


def gated_delta_rule(q, k, v, g, beta):
    B, S, H, dk = q.shape
    dv = v.shape[-1]
    C = DN_CHUNK
    N = S // C
    f32 = jnp.float32

    def chunks(t):
        t = t.astype(f32).reshape((B, N, C, H) + t.shape[3:])
        return t.transpose((1, 0, 3, 2) + tuple(range(4, t.ndim)))

    qc, kc, vc = chunks(q), chunks(k), chunks(v)
    gc = jnp.cumsum(chunks(g), axis=-1)
    bc = chunks(beta)
    kb = kc * bc[..., None]
    vb = vc * bc[..., None]
    incl = jnp.tril(jnp.ones((C, C), dtype=bool))
    strict = jnp.tril(jnp.ones((C, C), dtype=bool), -1)
    decay = jnp.exp(jnp.where(incl, gc[..., :, None] - gc[..., None, :], -jnp.inf))
    m = jnp.where(strict, jnp.einsum('nbhid,nbhjd->nbhij', kb, kc) * decay, 0.0)
    a = m + jnp.eye(C, dtype=f32)
    u_c = lax.linalg.triangular_solve(a, vb, left_side=True, lower=True, unit_diagonal=True)
    w_c = lax.linalg.triangular_solve(a, kb * jnp.exp(gc)[..., None], left_side=True,
                                      lower=True, unit_diagonal=True)
    qk = jnp.einsum('nbhid,nbhjd->nbhij', qc, kc) * decay

    def step(state, xs):
        q_i, k_i, u_i, w_i, g_i, qk_i = xs
        v_new = u_i - jnp.einsum('bhcd,bhde->bhce', w_i, state)
        o_i = (jnp.einsum('bhcd,bhde->bhce', q_i * jnp.exp(g_i)[..., None], state)
               + jnp.einsum('bhij,bhje->bhie', qk_i, v_new))
        g_last = g_i[..., -1]
        state = (state * jnp.exp(g_last)[..., None, None]
                 + jnp.einsum('bhcd,bhce->bhde',
                              k_i * jnp.exp(g_last[..., None] - g_i)[..., None], v_new))
        return state, o_i

    state0 = jnp.zeros((B, H, dk, dv), f32)
    _, o = lax.scan(step, state0, (qc, kc, u_c, w_c, gc, qk))
    return o.transpose(1, 0, 3, 2, 4).reshape(B, S, H, dv)


def dilated_window_branch(q, k, v, slopes, window, dilation):
    B, S, H, dh = q.shape
    r = dilation
    n = S // r
    span = window // r
    nb = -(-n // DA_BLOCK)
    n_pad = nb * DA_BLOCK
    z = B * r

    def to_sub(t):
        t = t.reshape(B, n, r, H, dh).transpose(0, 2, 1, 3, 4).reshape(z, n, H, dh)
        return jnp.pad(t, ((0, 0), (0, n_pad - n), (0, 0), (0, 0)))

    def band(t):
        tp = jnp.pad(t, ((0, 0), (DA_BLOCK, 0), (0, 0), (0, 0)))
        prev = tp[:, :n_pad].reshape(z, nb, DA_BLOCK, H, dh)
        cur = t.reshape(z, nb, DA_BLOCK, H, dh)
        return jnp.concatenate([prev, cur], axis=2)

    qs, ks, vs = to_sub(q), to_sub(k), to_sub(v)
    qb = qs.reshape(z, nb, DA_BLOCK, H, dh)
    kb, vb = band(ks), band(vs)
    s = jnp.einsum('znqhd,znkhd->znhqk', qb, kb).astype(jnp.float32) * (dh ** -0.5)
    qi = jnp.arange(DA_BLOCK)[:, None]
    ki = jnp.arange(2 * DA_BLOCK)[None, :]
    dist = qi + DA_BLOCK - ki
    key_pos = jnp.arange(nb)[:, None, None] * DA_BLOCK + ki[None] - DA_BLOCK
    valid = (dist[None] >= 0) & (dist[None] <= span) & (key_pos >= 0)
    bias = -slopes[:, None, None] * (dist * r).astype(jnp.float32)[None]
    s = jnp.where(valid[None, :, None], s + bias[None, None], -jnp.inf)
    mx = jnp.max(s, axis=-1, keepdims=True)
    p = jnp.exp(s - mx)
    l = jnp.sum(p, axis=-1, keepdims=True)
    o = jnp.einsum('znhqk,znkhd->znqhd', (p / l).astype(v.dtype), vb).astype(jnp.float32)
    lse = (mx + jnp.log(l))[..., 0]
    o = o.reshape(z, n_pad, H, dh)[:, :n].reshape(B, r, n, H, dh)
    o = o.transpose(0, 2, 1, 3, 4).reshape(B, S, H, dh)
    lse = lse.transpose(0, 1, 3, 2).reshape(z, n_pad, H)[:, :n].reshape(B, r, n, H)
    lse = lse.transpose(0, 2, 1, 3).reshape(B, S, H)
    return o, lse


def hybrid_mixer(u, w_in, conv_w, a_log, dt_bias, dn_norm, w_a, w_b, w_o):
    B, S, _ = u.shape
    proj = u @ w_in
    idx = np.cumsum(IN_SPLITS)[:-1].tolist()
    dn_qkv, dn_z, dn_b, dn_a, da_q, da_k, da_v, gate_a, gate_b = jnp.split(proj, idx, axis=-1)

    qkv = jax.nn.silu(causal_depthwise_conv(dn_qkv, conv_w))
    q, k, v = jnp.split(qkv, 3, axis=-1)
    q = l2norm(q.reshape(B, S, DN_HEADS, DN_HEAD_DIM)) * (DN_HEAD_DIM ** -0.5)
    k = l2norm(k.reshape(B, S, DN_HEADS, DN_HEAD_DIM))
    v = v.reshape(B, S, DN_HEADS, DN_HEAD_DIM)
    beta = jax.nn.sigmoid(dn_b.astype(jnp.float32))
    g = -jnp.exp(a_log.astype(jnp.float32)) * jax.nn.softplus(
        dn_a.astype(jnp.float32) + dt_bias.astype(jnp.float32))
    o_a = gated_delta_rule(q, k, v, g, beta).astype(u.dtype)
    o_a = rmsnorm(o_a, dn_norm) * jax.nn.silu(dn_z.reshape(B, S, DN_HEADS, DN_HEAD_DIM))
    y_a = o_a.reshape(B, S, DN_WIDTH) @ w_a

    qd = da_q.reshape(B, S, DA_HEADS, DA_HEAD_DIM)
    kd = da_k.reshape(B, S, DA_HEADS, DA_HEAD_DIM)
    vd = da_v.reshape(B, S, DA_HEADS, DA_HEAD_DIM)
    slopes = 2.0 ** (-ALIBI_MAX_EXP * jnp.arange(1, DA_HEADS + 1, dtype=jnp.float32) / DA_HEADS)
    outs, lses = [], []
    for window, dilation in DA_PATTERNS:
        o_p, lse_p = dilated_window_branch(qd, kd, vd, slopes, window, dilation)
        outs.append(o_p)
        lses.append(lse_p)
    wts = jax.nn.softmax(jnp.stack(lses, axis=0), axis=0)
    o_b = jnp.sum(wts[..., None] * jnp.stack(outs, axis=0), axis=0).astype(u.dtype)
    y_b = o_b.reshape(B, S, DA_WIDTH) @ w_b

    merged = jax.nn.sigmoid(gate_a) * y_a + jax.nn.sigmoid(gate_b) * y_b
    return merged @ w_o


def setup_inputs(seed: int = 0) -> dict:
    key = jax.random.key(seed)
    ks = jax.random.split(key, 24)
    f32 = jnp.float32
    D = D_MODEL

    def nrm(k, shape, scale):
        return jax.random.normal(k, shape, f32) * scale

    x = nrm(ks[0], (BATCH, SEQ, D), 1.0)
    c = nrm(ks[1], (BATCH, D), 1.0)
    ada_w = nrm(ks[2], (DEPTH, D, N_ADA * D), 0.5 * D ** -0.5)
    ada_b = nrm(ks[3], (DEPTH, N_ADA * D), 0.02)
    ln_ffn1 = 1.0 + nrm(ks[4], (DEPTH, D), 0.02)
    ln_mix = 1.0 + nrm(ks[5], (DEPTH, D), 0.02)
    ln_ffn2 = 1.0 + nrm(ks[6], (DEPTH, D), 0.02)
    ffn1_wg = nrm(ks[7], (DEPTH, D, D_FF), D ** -0.5)
    ffn1_wu = nrm(ks[8], (DEPTH, D, D_FF), D ** -0.5)
    ffn1_wd = nrm(ks[9], (DEPTH, D_FF, D), D_FF ** -0.5)
    w_in = nrm(ks[10], (DEPTH, D, IN_COLS), D ** -0.5)
    conv_w = nrm(ks[11], (DEPTH, DN_CONV, 3 * DN_WIDTH), DN_CONV ** -0.5)
    a_log = jnp.log(jax.random.uniform(ks[12], (DEPTH, DN_HEADS), f32, 1.0, 16.0))
    dt = jnp.exp(jax.random.uniform(ks[13], (DEPTH, DN_HEADS), f32,
                                    math.log(1e-3), math.log(1e-1)))
    dt_bias = dt + jnp.log(-jnp.expm1(-dt))
    dn_norm = 1.0 + nrm(ks[14], (DEPTH, DN_HEAD_DIM), 0.02)
    w_a = nrm(ks[15], (DEPTH, DN_WIDTH, D), DN_WIDTH ** -0.5)
    w_b = nrm(ks[16], (DEPTH, DA_WIDTH, D), DA_WIDTH ** -0.5)
    w_o = nrm(ks[17], (DEPTH, D, D), D ** -0.5)
    ffn2_wg = nrm(ks[18], (DEPTH, D, D_FF), D ** -0.5)
    ffn2_wu = nrm(ks[19], (DEPTH, D, D_FF), D ** -0.5)
    ffn2_wd = nrm(ks[20], (DEPTH, D_FF, D), D_FF ** -0.5)
    final_norm = 1.0 + nrm(ks[21], (D,), 0.02)
    return {"x": x, "c": c, "ada_w": ada_w, "ada_b": ada_b,
            "ln_ffn1": ln_ffn1, "ln_mix": ln_mix, "ln_ffn2": ln_ffn2,
            "ffn1_wg": ffn1_wg, "ffn1_wu": ffn1_wu, "ffn1_wd": ffn1_wd,
            "w_in": w_in, "conv_w": conv_w, "a_log": a_log, "dt_bias": dt_bias,
            "dn_norm": dn_norm, "w_a": w_a, "w_b": w_b, "w_o": w_o,
            "ffn2_wg": ffn2_wg, "ffn2_wu": ffn2_wu, "ffn2_wd": ffn2_wd,
            "final_norm": final_norm}


def reference(x, c, ada_w, ada_b, ln_ffn1, ln_mix, ln_ffn2, ffn1_wg, ffn1_wu, ffn1_wd,
              w_in, conv_w, a_log, dt_bias, dn_norm, w_a, w_b, w_o,
              ffn2_wg, ffn2_wu, ffn2_wd, final_norm):
    h = x
    c_act = jax.nn.silu(c)
    for l in range(DEPTH):
        mod = c_act @ ada_w[l] + ada_b[l]
        (sh1, sc1, gt1, sh2, sc2, gt2, sh3, sc3, gt3) = jnp.split(mod, N_ADA, axis=-1)
        f = swiglu(modulate(rmsnorm(h, ln_ffn1[l]), sh1, sc1), ffn1_wg[l], ffn1_wu[l], ffn1_wd[l])
        h = h + MACARON_WEIGHT * gt1[:, None, :] * f
        u = modulate(rmsnorm(h, ln_mix[l]), sh2, sc2)
        m = hybrid_mixer(u, w_in[l], conv_w[l], a_log[l], dt_bias[l], dn_norm[l],
                         w_a[l], w_b[l], w_o[l])
        h = h + gt2[:, None, :] * m
        f = swiglu(modulate(rmsnorm(h, ln_ffn2[l]), sh3, sc3), ffn2_wg[l], ffn2_wu[l], ffn2_wd[l])
        h = h + MACARON_WEIGHT * gt3[:, None, :] * f
    return rmsnorm(h, final_norm)
```

```python
import functools

import jax
import jax.numpy as jnp
import numpy as np
from jax import lax
from jax.experimental import pallas as pl
from jax.experimental.pallas import tpu as pltpu

F32 = jnp.float32
BF16 = jnp.bfloat16

D_MODEL = 1024
D_FF = 2816
DN_HEADS = 8
DN_HEAD_DIM = 128
DN_WIDTH = DN_HEADS * DN_HEAD_DIM
DN_CONV = 4
DN_CHUNK = 64
CHUNK_SHIFT = 6
DA_HEADS = 12
DA_HEAD_DIM = 64
DA_WIDTH = DA_HEADS * DA_HEAD_DIM
DA_PATTERNS = ((128, 1), (512, 4), (2048, 16))
DA_BLOCK = 128
ALIBI_MAX_EXP = 8.0
MACARON_WEIGHT = 0.5
NORM_EPS = 1e-6
N_ADA = 9
LANES = 128

COL_QKV = 0
COL_Z = 3 * DN_WIDTH
COL_DAQ = COL_Z + DN_WIDTH
COL_DAK = COL_DAQ + DA_WIDTH
COL_DAV = COL_DAK + DA_WIDTH
COL_GA = COL_DAV + DA_WIDTH
COL_GB = COL_GA + D_MODEL
COL_SMALL = COL_GB + D_MODEL
IN_COLS_PADDED = COL_SMALL + LANES

VMEM_LIMIT_BYTES = 56 * 1024 * 1024
NEG_BIG = -1e30

FFN_TM = 512
FFN_TF = 1408
INPROJ_TM = 256
DELTA_TB = 128
MERGE_TM = 512
ADA_TN = 1024


def _cparams(sem):
    return pltpu.CompilerParams(dimension_semantics=sem, vmem_limit_bytes=VMEM_LIMIT_BYTES)


def _dot(a, b):
    return jnp.dot(a, b, preferred_element_type=F32)


def _dot_nt(a, b):
    return lax.dot_general(a, b, (((1,), (1,)), ((), ())), preferred_element_type=F32)


def _split2(x):
    hi = x.astype(BF16)
    lo = (x - hi.astype(F32)).astype(BF16)
    return hi, lo


def _split3(x):
    hi = x.astype(BF16)
    r = x - hi.astype(F32)
    mid = r.astype(BF16)
    lo = (r - mid.astype(F32)).astype(BF16)
    return hi, mid, lo


def _mm3(a, b):
    ah, al = _split2(a)
    bh, bl = _split2(b)
    return _dot(ah, bh) + (_dot(ah, bl) + _dot(al, bh))


def _mm_exact_rhs(a, b_bf16):
    hi, mid, lo = _split3(a)
    return _dot(hi, b_bf16) + (_dot(mid, b_bf16) + _dot(lo, b_bf16))


def _rms_modulate(x, ln, sc, sh):
    ms = jnp.mean(x * x, axis=-1, keepdims=True)
    y = (x * lax.rsqrt(ms + NORM_EPS)) * ln
    return y * (1.0 + sc) + sh


def _ada_kernel(c_ref, w_ref, b_ref, o_ref):
    c = c_ref[...]
    ca = c * jax.nn.sigmoid(c)
    o_ref[...] = _mm3(ca, w_ref[...]) + b_ref[...]


def _ada_call(c, ada_w, ada_b):
    depth, d, n = ada_w.shape
    b = c.shape[0]
    return pl.pallas_call(
        _ada_kernel,
        grid=(depth, n // ADA_TN),
        in_specs=[
            pl.BlockSpec((b, d), lambda l, j: (0, 0)),
            pl.BlockSpec((None, d, ADA_TN), lambda l, j: (l, 0, j)),
            pl.BlockSpec((None, 1, ADA_TN), lambda l, j: (l, 0, j)),
        ],
        out_specs=pl.BlockSpec((None, b, ADA_TN), lambda l, j: (l, 0, j)),
        out_shape=jax.ShapeDtypeStruct((depth, b, n), F32),
        compiler_params=_cparams(("arbitrary", "arbitrary")),
        name="ada_mod",
    )(c, ada_w, ada_b.reshape(depth, 1, n))


def _ffn_kernel(x_ref, sh_ref, sc_ref, gt_ref, ln_ref, wg_ref, wu_ref, wd_ref, fn_ref, o_ref,
                xn_ref, acc_ref, *, n_ff_tiles, final_norm):
    j = pl.program_id(1)

    @pl.when(j == 0)
    def _():
        xn_ref[...] = _rms_modulate(x_ref[...], ln_ref[...], sc_ref[...], sh_ref[...]).astype(BF16)
        acc_ref[...] = jnp.zeros_like(acc_ref)

    xn = xn_ref[...]
    g = _dot(xn, wg_ref[...])
    u = _dot(xn, wu_ref[...])
    a = (g * jax.nn.sigmoid(g)) * u
    acc_ref[...] += _dot(a.astype(BF16), wd_ref[...])

    @pl.when(j == n_ff_tiles - 1)
    def _():
        hn = x_ref[...] + (MACARON_WEIGHT * gt_ref[...]) * acc_ref[...]
        if final_norm:
            ms = jnp.mean(hn * hn, axis=-1, keepdims=True)
            hn = (hn * lax.rsqrt(ms + NORM_EPS)) * fn_ref[...]
        o_ref[...] = hn


def _ffn_call(h, mod, ks, ln, wg, wu, wd, layer, seq, final_norm_w=None):
    t, d = h.shape
    tm = min(FFN_TM, seq)
    tiles_per_batch = seq // tm
    n_ff = D_FF // FFN_TF
    final = final_norm_w is not None
    fn = final_norm_w.reshape(1, d) if final else jnp.ones((1, d), F32)

    def mod_spec(k):
        return pl.BlockSpec((None, None, None, 1, d),
                            lambda i, j: (layer, k, i // tiles_per_batch, 0, 0))

    kern = functools.partial(_ffn_kernel, n_ff_tiles=n_ff, final_norm=final)
    return pl.pallas_call(
        kern,
        grid=(t // tm, n_ff),
        in_specs=[
            pl.BlockSpec((tm, d), lambda i, j: (i, 0)),
            mod_spec(ks[0]), mod_spec(ks[1]), mod_spec(ks[2]),
            pl.BlockSpec((None, 1, d), lambda i, j: (layer, 0, 0)),
            pl.BlockSpec((None, d, FFN_TF), lambda i, j: (layer, 0, j)),
            pl.BlockSpec((None, d, FFN_TF), lambda i, j: (layer, 0, j)),
            pl.BlockSpec((None, FFN_TF, d), lambda i, j: (layer, j, 0)),
            pl.BlockSpec((1, d), lambda i, j: (0, 0)),
        ],
        out_specs=pl.BlockSpec((tm, d), lambda i, j: (i, 0)),
        out_shape=jax.ShapeDtypeStruct((t, d), F32),
        scratch_shapes=[pltpu.VMEM((tm, d), BF16), pltpu.VMEM((tm, d), F32)],
        compiler_params=_cparams(("parallel", "arbitrary")),
        name="ffn",
    )(h, mod, mod, mod, ln, wg, wu, wd, fn)


def _inproj_kernel(x_ref, sh_ref, sc_ref, ln_ref, w_ref, cw_ref,
                   q_ref, k_ref, v_ref, z_ref, daq_ref, dak_ref, dav_ref, ga_ref, gb_ref, sm_ref,
                   cbuf_ref, *, tm, tiles_per_batch):
    i = pl.program_id(0)
    halo = 8

    @pl.when(i % tiles_per_batch == 0)
    def _():
        cbuf_ref[0:halo, :] = jnp.zeros((halo, 3 * DN_WIDTH), F32)

    xn = _rms_modulate(x_ref[...], ln_ref[...], sc_ref[...], sh_ref[...]).astype(BF16)
    cbuf_ref[halo:halo + tm, :] = _dot(xn, w_ref[:, COL_QKV:COL_Z])

    for cg in range(3 * DN_HEADS):
        c0 = cg * DN_HEAD_DIM
        cs = slice(c0, c0 + DN_HEAD_DIM)
        y = cbuf_ref[halo - 3:halo - 3 + tm, cs] * cw_ref[0:1, cs]
        for tap in range(1, DN_CONV):
            y = y + cbuf_ref[halo - 3 + tap:halo - 3 + tap + tm, cs] * cw_ref[tap:tap + 1, cs]
        y = y * jax.nn.sigmoid(y)
        head = cg % DN_HEADS
        hs = slice(head * DN_HEAD_DIM, (head + 1) * DN_HEAD_DIM)
        if cg < 2 * DN_HEADS:
            y = y * lax.rsqrt(jnp.sum(y * y, axis=-1, keepdims=True) + NORM_EPS)
        if cg < DN_HEADS:
            q_ref[:, hs] = (y * (DN_HEAD_DIM ** -0.5)).astype(BF16)
        elif cg < 2 * DN_HEADS:
            k_ref[:, hs] = y.astype(BF16)
        else:
            v_ref[:, hs] = y.astype(BF16)

    cbuf_ref[halo - 3:halo, :] = cbuf_ref[halo + tm - 3:halo + tm, :]

    z_ref[...] = _dot(xn, w_ref[:, COL_Z:COL_DAQ])
    daq_ref[...] = _dot(xn, w_ref[:, COL_DAQ:COL_DAK]).astype(BF16)
    dak_ref[...] = _dot(xn, w_ref[:, COL_DAK:COL_DAV]).astype(BF16)
    dav_ref[...] = _dot(xn, w_ref[:, COL_DAV:COL_GA]).astype(BF16)
    ga_ref[...] = _dot(xn, w_ref[:, COL_GA:COL_GB])
    gb_ref[...] = _dot(xn, w_ref[:, COL_GB:COL_SMALL])
    sm_ref[...] = _dot(xn, w_ref[:, COL_SMALL:IN_COLS_PADDED])


def _inproj_call(h, mod, ln, w_r, conv_w, layer, seq):
    t, d = h.shape
    tm = min(INPROJ_TM, seq)
    tiles_per_batch = seq // tm

    def mod_spec(k):
        return pl.BlockSpec((None, None, None, 1, d), lambda i: (layer, k, i // tiles_per_batch, 0, 0))

    def row_spec(n):
        return pl.BlockSpec((tm, n), lambda i: (i, 0))

    kern = functools.partial(_inproj_kernel, tm=tm, tiles_per_batch=tiles_per_batch)
    outs = [
        (DN_WIDTH, BF16), (DN_WIDTH, BF16), (DN_WIDTH, BF16), (DN_WIDTH, F32),
        (DA_WIDTH, BF16), (DA_WIDTH, BF16), (DA_WIDTH, BF16),
        (D_MODEL, F32), (D_MODEL, F32), (LANES, F32),
    ]
    return pl.pallas_call(
        kern,
        grid=(t // tm,),
        in_specs=[
            row_spec(d), mod_spec(3), mod_spec(4),
            pl.BlockSpec((None, 1, d), lambda i: (layer, 0, 0)),
            pl.BlockSpec((None, d, IN_COLS_PADDED), lambda i: (layer, 0, 0),
                         pipeline_mode=pl.Buffered(1)),
            pl.BlockSpec((None, DN_CONV, 3 * DN_WIDTH), lambda i: (layer, 0, 0)),
        ],
        out_specs=[row_spec(n) for n, _ in outs],
        out_shape=[jax.ShapeDtypeStruct((t, n), dt) for n, dt in outs],
        scratch_shapes=[pltpu.VMEM((tm + 8, 3 * DN_WIDTH), F32)],
        compiler_params=_cparams(("arbitrary",)),
        name="inproj",
    )(h, mod, mod, ln, w_r, conv_w)


def _delta_kernel(q_ref, k_ref, v_ref, sm_ref, z_ref, prm_ref, nrm_ref, o_ref, state_ref, *, tb):
    t_idx = pl.program_id(1)
    n_chunks = tb // DN_CHUNK

    @pl.when(t_idx == 0)
    def _():
        state_ref[...] = jnp.zeros_like(state_ref)

    ri = lax.broadcasted_iota(jnp.int32, (tb, tb), 0)
    ci = lax.broadcasted_iota(jnp.int32, (tb, tb), 1)
    r_chunk = ri >> CHUNK_SHIFT
    c_chunk = ci >> CHUNK_SHIFT
    same_chunk = r_chunk == c_chunk
    incl = same_chunk & (ci <= ri)
    strict = same_chunk & (ci < ri)
    one = jnp.ones((tb, tb), F32)
    zero = jnp.zeros((tb, tb), F32)
    lower_incl = jnp.where(incl, one, zero).astype(BF16)
    upper_strict = jnp.where(same_chunk & (ci > ri), one, zero).astype(BF16)
    upper_incl = jnp.where(same_chunk & (ri <= ci), one, zero).astype(BF16)

    sm = sm_ref[...]
    beta_full = jax.nn.sigmoid(sm)
    xs = sm + prm_ref[1:2, :]
    softplus = jnp.maximum(xs, 0.0) + jnp.log1p(jnp.exp(-jnp.abs(xs)))
    g_full = (-jnp.exp(prm_ref[0:1, :])) * softplus
    g_t = g_full.T

    gh, gm, gl = _split3(g_full)
    gc_col = _dot(lower_incl, gh) + (_dot(lower_incl, gm) + _dot(lower_incl, gl))
    rem_col = _dot(upper_strict, gh) + (_dot(upper_strict, gm) + _dot(upper_strict, gl))
    gc_row = _mm_exact_rhs(g_t, upper_incl)
    tot_rows = []
    for c in range(n_chunks):
        sel = jnp.where(r_chunk == c, one, zero).astype(BF16)
        tot_rows.append(_mm_exact_rhs(g_t, sel))

    eye = jnp.where(ri == ci, one, zero)
    lane_chunk = [c_chunk == c for c in range(n_chunks)]
    nrm = nrm_ref[...]

    for h in range(DN_HEADS):
        hs = slice(h * DN_HEAD_DIM, (h + 1) * DN_HEAD_DIM)
        gl_ = DN_HEADS + h
        qh = q_ref[:, hs]
        kh = k_ref[:, hs]
        kf = kh.astype(F32)
        vf = v_ref[:, hs].astype(F32)
        qf = qh.astype(F32)
        beta = beta_full[:, h:h + 1]
        gcc = gc_col[:, gl_:gl_ + 1]
        gcr = gc_row[gl_:gl_ + 1, :]
        rem = rem_col[:, gl_:gl_ + 1]

        kb = kf * beta
        vb = vf * beta
        dec = jnp.exp(jnp.where(incl, gcc - gcr, NEG_BIG))
        m = jnp.where(strict, _dot_nt(kb.astype(BF16), kh) * dec, 0.0)
        qk = _dot_nt(qh, kh) * dec

        x_inv = eye - jnp.where(((ri ^ ci) == 1) & ((ri & 1) == 1), m, 0.0)
        for level in range(1, CHUNK_SHIFT):
            s = 1 << level
            blk = ((ri >> (level + 1)) == (ci >> (level + 1))) & ((ri & s) != 0) & ((ci & s) == 0)
            p = _mm3(jnp.where(blk, m, 0.0), x_inv)
            x_inv = x_inv - _mm3(x_inv, p)

        egc = jnp.exp(gcc)
        u = _mm3(x_inv, vb)
        w = _mm3(x_inv, kb * egc)
        qg = (qf * egc).astype(BF16)
        kg_t = (kf * jnp.exp(rem)).T

        st = state_ref[h]
        o_inter = []
        vn_all = jnp.zeros((tb, DN_HEAD_DIM), F32)
        for c in range(n_chunks):
            r0 = c * DN_CHUNK
            st_b = st.astype(BF16)
            vn = u[r0:r0 + DN_CHUNK] - _dot(w[r0:r0 + DN_CHUNK].astype(BF16), st_b)
            o_inter.append(_dot(qg[r0:r0 + DN_CHUNK], st_b))
            pieces = [vn]
            if r0:
                pieces.insert(0, jnp.zeros((r0, DN_HEAD_DIM), F32))
            if tb - r0 - DN_CHUNK:
                pieces.append(jnp.zeros((tb - r0 - DN_CHUNK, DN_HEAD_DIM), F32))
            vn_pad = jnp.concatenate(pieces, axis=0) if len(pieces) > 1 else vn
            vn_all = vn_all + vn_pad
            kg_c = jnp.where(lane_chunk[c], kg_t, 0.0).astype(BF16)
            st = st * jnp.exp(tot_rows[c][gl_:gl_ + 1, :]) + _dot(kg_c, vn_pad.astype(BF16))
        state_ref[h] = st

        o = jnp.concatenate(o_inter, axis=0) + _dot(qk.astype(BF16), vn_all.astype(BF16))
        ms = jnp.mean(o * o, axis=-1, keepdims=True)
        on = (o * lax.rsqrt(ms + NORM_EPS)) * nrm
        zh = z_ref[:, hs]
        o_ref[:, hs] = (on * (zh * jax.nn.sigmoid(zh))).astype(BF16)


def _delta_call(q, k, v, small, z, prm, nrm, batch, seq):
    t = q.shape[0]
    tb = min(DELTA_TB, seq)
    nt = seq // tb

    def row_spec(n):
        return pl.BlockSpec((tb, n), lambda b, i: (b * nt + i, 0))

    kern = functools.partial(_delta_kernel, tb=tb)
    return pl.pallas_call(
        kern,
        grid=(batch, nt),
        in_specs=[
            row_spec(DN_WIDTH), row_spec(DN_WIDTH), row_spec(DN_WIDTH), row_spec(LANES),
            row_spec(DN_WIDTH),
            pl.BlockSpec((8, LANES), lambda b, i: (0, 0)),
            pl.BlockSpec((1, DN_HEAD_DIM), lambda b, i: (0, 0)),
        ],
        out_specs=row_spec(DN_WIDTH),
        out_shape=jax.ShapeDtypeStruct((t, DN_WIDTH), BF16),
        scratch_shapes=[pltpu.VMEM((DN_HEADS, DN_HEAD_DIM, DN_HEAD_DIM), F32)],
        compiler_params=_cparams(("parallel", "arbitrary")),
        name="delta_rule",
    )(q, k, v, small, z, prm, nrm)


def _attn_kernel(q_ref, kp_ref, kc_ref, vp_ref, vc_ref, o_ref, lse_ref, *, dilation):
    blk = pl.program_id(2)
    qi = lax.broadcasted_iota(jnp.int32, (DA_BLOCK, DA_BLOCK), 0)
    ki = lax.broadcasted_iota(jnp.int32, (DA_BLOCK, DA_BLOCK), 1)
    valid_prev = (ki >= qi) & (blk > 0)
    valid_cur = ki <= qi
    dist_prev = ((qi - ki + DA_BLOCK) * dilation).astype(F32)
    dist_cur = ((qi - ki) * dilation).astype(F32)
    lane = lax.broadcasted_iota(jnp.int32, (DA_BLOCK, LANES), 1)
    lse_tile = jnp.zeros((DA_BLOCK, LANES), F32)
    scale = DA_HEAD_DIM ** -0.5

    for h in range(DA_HEADS):
        hs = slice(h * DA_HEAD_DIM, (h + 1) * DA_HEAD_DIM)
        slope = 2.0 ** (-ALIBI_MAX_EXP * (h + 1) / DA_HEADS)
        qh = q_ref[:, hs]
        sp = _dot_nt(qh, kp_ref[:, hs]) * scale
        sc = _dot_nt(qh, kc_ref[:, hs]) * scale
        sp = jnp.where(valid_prev, sp + (-slope) * dist_prev, NEG_BIG)
        sc = jnp.where(valid_cur, sc + (-slope) * dist_cur, NEG_BIG)
        mx = jnp.maximum(jnp.max(sp, axis=-1, keepdims=True), jnp.max(sc, axis=-1, keepdims=True))
        pp = jnp.exp(sp - mx)
        pc = jnp.exp(sc - mx)
        l = jnp.sum(pp, axis=-1, keepdims=True) + jnp.sum(pc, axis=-1, keepdims=True)
        acc = _dot(pp.astype(BF16), vp_ref[:, hs]) + _dot(pc.astype(BF16), vc_ref[:, hs])
        o_ref[:, hs] = (acc / l).astype(BF16)
        lse_tile = jnp.where(lane == h, mx + jnp.log(l), lse_tile)
    lse_ref[...] = lse_tile


def _attn_call(q, k, v, batch, seq, dilation):
    r = dilation
    n = seq // r
    nb = n // DA_BLOCK
    w = DA_WIDTH
    qv = q.reshape(batch, n, r * w)
    kv = k.reshape(batch, n, r * w)
    vv = v.reshape(batch, n, r * w)

    cur = pl.BlockSpec((None, DA_BLOCK, w), lambda b, p, i: (b, i, p))
    prev = pl.BlockSpec((None, DA_BLOCK, w), lambda b, p, i: (b, jnp.maximum(i - 1, 0), p))
    kern = functools.partial(_attn_kernel, dilation=r)
    o, lse = pl.pallas_call(
        kern,
        grid=(batch, r, nb),
        in_specs=[cur, prev, cur, prev, cur],
        out_specs=[cur, pl.BlockSpec((None, DA_BLOCK, LANES), lambda b, p, i: (b, i, p))],
        out_shape=[jax.ShapeDtypeStruct((batch, n, r * w), BF16),
                   jax.ShapeDtypeStruct((batch, n, r * LANES), F32)],
        compiler_params=_cparams(("parallel", "parallel", "arbitrary")),
        name=f"dilated_attn_r{r}",
    )(qv, kv, kv, vv, vv)
    return o.reshape(batch * seq, w), lse.reshape(batch * seq, LANES)


def _merge_kernel(h_ref, gt_ref, oa_ref, o1_ref, o2_ref, o3_ref, l1_ref, l2_ref, l3_ref,
                  ga_ref, gb_ref, wa_ref, wb_ref, wo_ref, ex_ref, out_ref):
    l1 = l1_ref[...]
    l2 = l2_ref[...]
    l3 = l3_ref[...]
    mx = jnp.maximum(jnp.maximum(l1, l2), l3)
    e1 = jnp.exp(l1 - mx)
    e2 = jnp.exp(l2 - mx)
    e3 = jnp.exp(l3 - mx)
    den = (e1 + e2) + e3
    ex = ex_ref[...]
    w1 = _mm_exact_rhs(e1 / den, ex)
    w2 = _mm_exact_rhs(e2 / den, ex)
    w3 = _mm_exact_rhs(e3 / den, ex)
    ob = (w1 * o1_ref[...].astype(F32) + w2 * o2_ref[...].astype(F32)) + w3 * o3_ref[...].astype(F32)
    yb = _dot(ob.astype(BF16), wb_ref[...])
    ya = _dot(oa_ref[...], wa_ref[...])
    merged = jax.nn.sigmoid(ga_ref[...]) * ya + jax.nn.sigmoid(gb_ref[...]) * yb
    mo = _dot(merged.astype(BF16), wo_ref[...])
    out_ref[...] = h_ref[...] + gt_ref[...] * mo


def _merge_call(h, mod, oa, outs, lses, ga, gb, wa, wb, wo, expand, layer, seq):
    t, d = h.shape
    tm = min(MERGE_TM, seq)
    tiles_per_batch = seq // tm

    def row_spec(n):
        return pl.BlockSpec((tm, n), lambda i: (i, 0))

    return pl.pallas_call(
        _merge_kernel,
        grid=(t // tm,),
        in_specs=[
            row_spec(d),
            pl.BlockSpec((None, None, None, 1, d), lambda i: (layer, 5, i // tiles_per_batch, 0, 0)),
            row_spec(DN_WIDTH),
            row_spec(DA_WIDTH), row_spec(DA_WIDTH), row_spec(DA_WIDTH),
            row_spec(LANES), row_spec(LANES), row_spec(LANES),
            row_spec(d), row_spec(d),
            pl.BlockSpec((None, DN_WIDTH, d), lambda i: (layer, 0, 0)),
            pl.BlockSpec((None, DA_WIDTH, d), lambda i: (layer, 0, 0)),
            pl.BlockSpec((None, d, d), lambda i: (layer, 0, 0)),
            pl.BlockSpec((LANES, DA_WIDTH), lambda i: (0, 0)),
        ],
        out_specs=row_spec(d),
        out_shape=jax.ShapeDtypeStruct((t, d), F32),
        compiler_params=_cparams(("parallel",)),
        name="merge_out",
    )(h, mod, oa, outs[0], outs[1], outs[2], lses[0], lses[1], lses[2], ga, gb, wa, wb, wo, expand)


def _reorder_w_in(w_in):
    o_small = 4 * DN_WIDTH
    main_a = w_in[:, :, :o_small]
    small = w_in[:, :, o_small:o_small + 2 * DN_HEADS]
    main_b = w_in[:, :, o_small + 2 * DN_HEADS:]
    pad = jnp.zeros(w_in.shape[:2] + (LANES - 2 * DN_HEADS,), w_in.dtype)
    return jnp.concatenate([main_a, main_b, small, pad], axis=-1).astype(BF16)


def kernel(x, c, ada_w, ada_b, ln_ffn1, ln_mix, ln_ffn2, ffn1_wg, ffn1_wu, ffn1_wd, w_in, conv_w,
           a_log, dt_bias, dn_norm, w_a, w_b, w_o, ffn2_wg, ffn2_wu, ffn2_wd, final_norm):
    batch, seq, d = x.shape
    depth = ada_w.shape[0]
    t = batch * seq

    mod = _ada_call(c, ada_w, ada_b)
    mod = mod.reshape(depth, batch, N_ADA, 1, d).transpose(0, 2, 1, 3, 4)

    w_r = _reorder_w_in(w_in)
    bf = lambda a: a.astype(BF16)
    f1g, f1u, f1d = bf(ffn1_wg), bf(ffn1_wu), bf(ffn1_wd)
    f2g, f2u, f2d = bf(ffn2_wg), bf(ffn2_wu), bf(ffn2_wd)
    wa, wb, wo = bf(w_a), bf(w_b), bf(w_o)
    ln1 = ln_ffn1.reshape(depth, 1, d)
    lnm = ln_mix.reshape(depth, 1, d)
    ln2 = ln_ffn2.reshape(depth, 1, d)

    lane = np.arange(LANES)
    expand = jnp.asarray((lane[:, None] == (np.arange(DA_WIDTH)[None, :] // DA_HEAD_DIM)), BF16)
    zpad = jnp.zeros((depth, DN_HEADS), F32)
    prm_rows = jnp.stack([jnp.concatenate([zpad, a_log.astype(F32)], axis=-1),
                          jnp.concatenate([zpad, dt_bias.astype(F32)], axis=-1)], axis=1)
    prm = jnp.pad(prm_rows, ((0, 0), (0, 6), (0, LANES - 2 * DN_HEADS)))

    h = x.reshape(t, d)
    for l in range(depth):
        h = _ffn_call(h, mod, (0, 1, 2), ln1, f1g, f1u, f1d, l, seq)
        q, k, v, z, daq, dak, dav, ga, gb, small = _inproj_call(h, mod, lnm, w_r, conv_w, l, seq)
        oa = _delta_call(q, k, v, small, z, prm[l], dn_norm[l].reshape(1, DN_HEAD_DIM), batch, seq)
        outs, lses = [], []
        for _, dilation in DA_PATTERNS:
            o_p, lse_p = _attn_call(daq, dak, dav, batch, seq, dilation)
            outs.append(o_p)
            lses.append(lse_p)
        h = _merge_call(h, mod, oa, outs, lses, ga, gb, wa, wb, wo, expand, l, seq)
        h = _ffn_call(h, mod, (6, 7, 8), ln2, f2g, f2u, f2d, l, seq,
                      final_norm_w=final_norm if l == depth - 1 else None)
    return h.reshape(batch, seq, d)
```

```python
import functools

import jax
import jax.numpy as jnp
import numpy as np
from jax import lax
from jax.experimental import pallas as pl
from jax.experimental.pallas import tpu as pltpu

F32 = jnp.float32
BF16 = jnp.bfloat16

D_MODEL = 1024
D_FF = 2816
DN_HEADS = 8
DN_HEAD_DIM = 128
DN_WIDTH = DN_HEADS * DN_HEAD_DIM
DN_CONV = 4
DN_CHUNK = 64
CHUNK_SHIFT = 6
DA_HEADS = 12
DA_HEAD_DIM = 64
DA_WIDTH = DA_HEADS * DA_HEAD_DIM
DA_PATTERNS = ((128, 1), (512, 4), (2048, 16))
DA_BLOCK = 128
ALIBI_MAX_EXP = 8.0
MACARON_WEIGHT = 0.5
NORM_EPS = 1e-6
N_ADA = 9
LANES = 128

COL_QKV = 0
COL_Z = 3 * DN_WIDTH
COL_DAQ = COL_Z + DN_WIDTH
COL_DAK = COL_DAQ + DA_WIDTH
COL_DAV = COL_DAK + DA_WIDTH
COL_GA = COL_DAV + DA_WIDTH
COL_GB = COL_GA + D_MODEL
COL_SMALL = COL_GB + D_MODEL
IN_COLS_PADDED = COL_SMALL + LANES

VMEM_LIMIT_BYTES = 56 * 1024 * 1024
NEG_BIG = -1e30

FFN_TM = 512
FFN_TF = 1408
INPROJ_TM = 256
DELTA_TB = 256
PAIR = 2 * DN_CHUNK
MERGE_TM = 512
ADA_TN = 1024


def _cparams(sem):
    return pltpu.CompilerParams(dimension_semantics=sem, vmem_limit_bytes=VMEM_LIMIT_BYTES)


def _dot(a, b):
    return jnp.dot(a, b, preferred_element_type=F32)


def _dot_nt(a, b):
    return lax.dot_general(a, b, (((1,), (1,)), ((), ())), preferred_element_type=F32)


def _bmm(a, b):
    return lax.dot_general(a, b, (((2,), (1,)), ((0,), (0,))), preferred_element_type=F32)


def _bmm_nt(a, b):
    return lax.dot_general(a, b, (((2,), (2,)), ((0,), (0,))), preferred_element_type=F32)


def _split2(x):
    hi = x.astype(BF16)
    lo = (x - hi.astype(F32)).astype(BF16)
    return hi, lo


def _split3(x):
    hi = x.astype(BF16)
    r = x - hi.astype(F32)
    mid = r.astype(BF16)
    lo = (r - mid.astype(F32)).astype(BF16)
    return hi, mid, lo


def _mm3(a, b):
    ah, al = _split2(a)
    bh, bl = _split2(b)
    return _dot(ah, bh) + (_dot(ah, bl) + _dot(al, bh))


def _mm_exact_rhs(a, b_bf16):
    hi, mid, lo = _split3(a)
    return _dot(hi, b_bf16) + (_dot(mid, b_bf16) + _dot(lo, b_bf16))


def _rms_modulate(x, ln, sc, sh):
    ms = jnp.mean(x * x, axis=-1, keepdims=True)
    y = (x * lax.rsqrt(ms + NORM_EPS)) * ln
    return y * (1.0 + sc) + sh


def _ada_kernel(c_ref, w_ref, b_ref, o_ref):
    c = c_ref[...]
    ca = c * jax.nn.sigmoid(c)
    o_ref[...] = _mm3(ca, w_ref[...]) + b_ref[...]


def _ada_call(c, ada_w, ada_b):
    depth, d, n = ada_w.shape
    b = c.shape[0]
    return pl.pallas_call(
        _ada_kernel,
        grid=(depth, n // ADA_TN),
        in_specs=[
            pl.BlockSpec((b, d), lambda l, j: (0, 0)),
            pl.BlockSpec((None, d, ADA_TN), lambda l, j: (l, 0, j)),
            pl.BlockSpec((None, 1, ADA_TN), lambda l, j: (l, 0, j)),
        ],
        out_specs=pl.BlockSpec((None, b, ADA_TN), lambda l, j: (l, 0, j)),
        out_shape=jax.ShapeDtypeStruct((depth, b, n), F32),
        compiler_params=_cparams(("arbitrary", "arbitrary")),
        name="ada_mod",
    )(c, ada_w, ada_b.reshape(depth, 1, n))


def _ffn_kernel(x_ref, sh_ref, sc_ref, gt_ref, ln_ref, wg_ref, wu_ref, wd_ref, fn_ref, o_ref,
                xn_ref, acc_ref, *, n_ff_tiles, final_norm):
    j = pl.program_id(1)

    @pl.when(j == 0)
    def _():
        xn_ref[...] = _rms_modulate(x_ref[...], ln_ref[...], sc_ref[...], sh_ref[...]).astype(BF16)
        acc_ref[...] = jnp.zeros_like(acc_ref)

    xn = xn_ref[...]
    g = _dot(xn, wg_ref[...])
    u = _dot(xn, wu_ref[...])
    a = (g * jax.nn.sigmoid(g)) * u
    acc_ref[...] += _dot(a.astype(BF16), wd_ref[...])

    @pl.when(j == n_ff_tiles - 1)
    def _():
        hn = x_ref[...] + (MACARON_WEIGHT * gt_ref[...]) * acc_ref[...]
        if final_norm:
            ms = jnp.mean(hn * hn, axis=-1, keepdims=True)
            hn = (hn * lax.rsqrt(ms + NORM_EPS)) * fn_ref[...]
        o_ref[...] = hn


def _ffn_call(h, mod, ks, ln, wg, wu, wd, layer, seq, final_norm_w=None):
    t, d = h.shape
    tm = min(FFN_TM, seq)
    tiles_per_batch = seq // tm
    n_ff = D_FF // FFN_TF
    final = final_norm_w is not None
    fn = final_norm_w.reshape(1, d) if final else jnp.ones((1, d), F32)

    def mod_spec(k):
        return pl.BlockSpec((None, None, None, 1, d),
                            lambda i, j: (layer, k, i // tiles_per_batch, 0, 0))

    kern = functools.partial(_ffn_kernel, n_ff_tiles=n_ff, final_norm=final)
    return pl.pallas_call(
        kern,
        grid=(t // tm, n_ff),
        in_specs=[
            pl.BlockSpec((tm, d), lambda i, j: (i, 0)),
            mod_spec(ks[0]), mod_spec(ks[1]), mod_spec(ks[2]),
            pl.BlockSpec((None, 1, d), lambda i, j: (layer, 0, 0)),
            pl.BlockSpec((None, d, FFN_TF), lambda i, j: (layer, 0, j)),
            pl.BlockSpec((None, d, FFN_TF), lambda i, j: (layer, 0, j)),
            pl.BlockSpec((None, FFN_TF, d), lambda i, j: (layer, j, 0)),
            pl.BlockSpec((1, d), lambda i, j: (0, 0)),
        ],
        out_specs=pl.BlockSpec((tm, d), lambda i, j: (i, 0)),
        out_shape=jax.ShapeDtypeStruct((t, d), F32),
        scratch_shapes=[pltpu.VMEM((tm, d), BF16), pltpu.VMEM((tm, d), F32)],
        compiler_params=_cparams(("parallel", "arbitrary")),
        name="ffn",
    )(h, mod, mod, mod, ln, wg, wu, wd, fn)


def _inproj_kernel(x_ref, sh_ref, sc_ref, ln_ref, w_ref, cw_ref,
                   q_ref, k_ref, v_ref, z_ref, daq_ref, dak_ref, dav_ref, ga_ref, gb_ref, sm_ref,
                   cbuf_ref, *, tm, tiles_per_batch):
    i = pl.program_id(0)
    halo = 8

    @pl.when(i % tiles_per_batch == 0)
    def _():
        cbuf_ref[0:halo, :] = jnp.zeros((halo, 3 * DN_WIDTH), F32)

    xn = _rms_modulate(x_ref[...], ln_ref[...], sc_ref[...], sh_ref[...]).astype(BF16)
    cbuf_ref[halo:halo + tm, :] = _dot(xn, w_ref[:, COL_QKV:COL_Z])

    for cg in range(3 * DN_HEADS):
        c0 = cg * DN_HEAD_DIM
        cs = slice(c0, c0 + DN_HEAD_DIM)
        y = cbuf_ref[halo - 3:halo - 3 + tm, cs] * cw_ref[0:1, cs]
        for tap in range(1, DN_CONV):
            y = y + cbuf_ref[halo - 3 + tap:halo - 3 + tap + tm, cs] * cw_ref[tap:tap + 1, cs]
        y = y * jax.nn.sigmoid(y)
        head = cg % DN_HEADS
        if cg < 2 * DN_HEADS:
            y = y * lax.rsqrt(jnp.sum(y * y, axis=-1, keepdims=True) + NORM_EPS)
        if cg < DN_HEADS:
            q_ref[head] = (y * (DN_HEAD_DIM ** -0.5)).astype(BF16)
        elif cg < 2 * DN_HEADS:
            k_ref[head] = y.astype(BF16)
        else:
            v_ref[head] = y.astype(BF16)

    cbuf_ref[halo - 3:halo, :] = cbuf_ref[halo + tm - 3:halo + tm, :]

    z_ref[...] = _dot(xn, w_ref[:, COL_Z:COL_DAQ])
    daq_ref[...] = _dot(xn, w_ref[:, COL_DAQ:COL_DAK]).astype(BF16)
    dak_ref[...] = _dot(xn, w_ref[:, COL_DAK:COL_DAV]).astype(BF16)
    dav_ref[...] = _dot(xn, w_ref[:, COL_DAV:COL_GA]).astype(BF16)
    ga_ref[...] = _dot(xn, w_ref[:, COL_GA:COL_GB])
    gb_ref[...] = _dot(xn, w_ref[:, COL_GB:COL_SMALL])
    sm_ref[...] = _dot(xn, w_ref[:, COL_SMALL:IN_COLS_PADDED])


def _inproj_call(h, mod, ln, w_r, conv_w, layer, seq):
    t, d = h.shape
    tm = min(INPROJ_TM, seq)
    tiles_per_batch = seq // tm

    def mod_spec(k):
        return pl.BlockSpec((None, None, None, 1, d), lambda i: (layer, k, i // tiles_per_batch, 0, 0))

    def row_spec(n):
        return pl.BlockSpec((tm, n), lambda i: (i, 0))

    kern = functools.partial(_inproj_kernel, tm=tm, tiles_per_batch=tiles_per_batch)
    outs = [
        (DN_WIDTH, F32),
        (DA_WIDTH, BF16), (DA_WIDTH, BF16), (DA_WIDTH, BF16),
        (D_MODEL, F32), (D_MODEL, F32), (LANES, F32),
    ]
    batch = t // seq
    head_spec = pl.BlockSpec((None, DN_HEADS, tm, DN_HEAD_DIM),
                             lambda i: (i // tiles_per_batch, 0, i % tiles_per_batch, 0))
    head_shape = jax.ShapeDtypeStruct((batch, DN_HEADS, seq, DN_HEAD_DIM), BF16)
    return pl.pallas_call(
        kern,
        grid=(t // tm,),
        in_specs=[
            row_spec(d), mod_spec(3), mod_spec(4),
            pl.BlockSpec((None, 1, d), lambda i: (layer, 0, 0)),
            pl.BlockSpec((None, d, IN_COLS_PADDED), lambda i: (layer, 0, 0),
                         pipeline_mode=pl.Buffered(1)),
            pl.BlockSpec((None, DN_CONV, 3 * DN_WIDTH), lambda i: (layer, 0, 0)),
        ],
        out_specs=[head_spec] * 3 + [row_spec(n) for n, _ in outs],
        out_shape=[head_shape] * 3 + [jax.ShapeDtypeStruct((t, n), dt) for n, dt in outs],
        scratch_shapes=[pltpu.VMEM((tm + 8, 3 * DN_WIDTH), F32)],
        compiler_params=_cparams(("arbitrary",)),
        name="inproj",
    )(h, mod, mod, ln, w_r, conv_w)


def _delta_kernel(q_ref, k_ref, v_ref, sm_ref, z_ref, prm_ref, nrm_ref, o_ref, state_ref, *, tb):
    ns = tb // PAIR
    gn = DN_HEADS * ns
    dh = DN_HEAD_DIM
    t_idx = pl.program_id(1)

    @pl.when(t_idx == 0)
    def _():
        state_ref[...] = jnp.zeros_like(state_ref)

    ri = lax.broadcasted_iota(jnp.int32, (PAIR, PAIR), 0)
    ci = lax.broadcasted_iota(jnp.int32, (PAIR, PAIR), 1)
    r_chunk = ri >> CHUNK_SHIFT
    c_chunk = ci >> CHUNK_SHIFT
    same_chunk = r_chunk == c_chunk
    incl = same_chunk & (ci <= ri)
    strict = same_chunk & (ci < ri)
    one = jnp.ones((PAIR, PAIR), F32)
    zero = jnp.zeros((PAIR, PAIR), F32)
    lower_incl = jnp.where(incl, one, zero).astype(BF16)
    upper_strict = jnp.where(same_chunk & (ci > ri), one, zero).astype(BF16)
    upper_incl = jnp.where(same_chunk & (ri <= ci), one, zero).astype(BF16)
    chunk_sel = [jnp.where(r_chunk == c, one, zero).astype(BF16) for c in range(2)]

    sm = sm_ref[...]
    beta_full = jax.nn.sigmoid(sm)
    xs = sm + prm_ref[1:2, :]
    softplus = jnp.maximum(xs, 0.0) + jnp.log1p(jnp.exp(-jnp.abs(xs)))
    g_full = (-jnp.exp(prm_ref[0:1, :])) * softplus

    beta_b, gcc_b, rem_b, gcr_b = [], [], [], []
    tot_b = [[None, None] for _ in range(ns)]
    per_sub = []
    for s in range(ns):
        rows = slice(s * PAIR, (s + 1) * PAIR)
        g_s = g_full[rows]
        gh, gm, gl = _split3(g_s)
        gc_col = _dot(lower_incl, gh) + (_dot(lower_incl, gm) + _dot(lower_incl, gl))
        rem_col = _dot(upper_strict, gh) + (_dot(upper_strict, gm) + _dot(upper_strict, gl))
        g_t = g_s.T
        gc_row = _mm_exact_rhs(g_t, upper_incl)
        tots = [_mm_exact_rhs(g_t, chunk_sel[c]) for c in range(2)]
        per_sub.append((beta_full[rows], gc_col, rem_col, gc_row, tots))
    for h in range(DN_HEADS):
        gl_ = DN_HEADS + h
        for s in range(ns):
            beta_s, gc_col, rem_col, gc_row, tots = per_sub[s]
            beta_b.append(jnp.broadcast_to(beta_s[:, h:h + 1], (PAIR, dh)))
            gcc_b.append(jnp.broadcast_to(gc_col[:, gl_:gl_ + 1], (PAIR, PAIR)))
            rem_b.append(jnp.broadcast_to(rem_col[:, gl_:gl_ + 1], (PAIR, dh)))
            gcr_b.append(jnp.broadcast_to(gc_row[gl_:gl_ + 1, :], (PAIR, PAIR)))
    for s in range(ns):
        for c in range(2):
            tot_b[s][c] = jnp.stack(
                [jnp.broadcast_to(per_sub[s][4][c][DN_HEADS + h:DN_HEADS + h + 1, :], (dh, dh))
                 for h in range(DN_HEADS)], axis=0)
    beta_b = jnp.stack(beta_b, axis=0)
    gcc_b = jnp.stack(gcc_b, axis=0)
    rem_b = jnp.stack(rem_b, axis=0)
    gcr_b = jnp.stack(gcr_b, axis=0)

    q3 = q_ref[...].reshape(gn, PAIR, dh)
    k3 = k_ref[...].reshape(gn, PAIR, dh)
    qf = q3.astype(F32)
    kf = k3.astype(F32)
    vf = v_ref[...].reshape(gn, PAIR, dh).astype(F32)
    kb = kf * beta_b
    vb = vf * beta_b
    dec = jnp.exp(jnp.where(incl[None], gcc_b - gcr_b, NEG_BIG))
    m = jnp.where(strict[None], _bmm_nt(kb.astype(BF16), k3) * dec, 0.0)
    qk = _bmm_nt(q3, k3) * dec

    y = -jnp.where((((ri ^ ci) == 1) & ((ri & 1) == 1))[None], m, 0.0)
    for level in range(1, CHUNK_SHIFT):
        s_ = 1 << level
        blk = ((ri >> (level + 1)) == (ci >> (level + 1))) & ((ri & s_) != 0) & ((ci & s_) == 0)
        c_off = jnp.where(blk[None], m, 0.0)
        p = c_off + _bmm(c_off.astype(BF16), y.astype(BF16))
        y = (y - p) - _bmm(y.astype(BF16), p.astype(BF16))

    egc = jnp.exp(gcc_b)
    rhs = jnp.concatenate([vb, kb * egc], axis=-1)
    uw = rhs + _bmm(y.astype(BF16), rhs.astype(BF16))
    u4 = uw[:, :, :dh].reshape(DN_HEADS, ns, PAIR, dh)
    w4 = uw[:, :, dh:].astype(BF16).reshape(DN_HEADS, ns, PAIR, dh)
    qg4 = (qf * egc).astype(BF16).reshape(DN_HEADS, ns, PAIR, dh)
    kg_t4 = jnp.swapaxes(kf * jnp.exp(rem_b), 1, 2).reshape(DN_HEADS, ns, dh, PAIR)
    lane_chunk = [(c_chunk == c)[None] for c in range(2)]
    zeros_half = jnp.zeros((DN_HEADS, DN_CHUNK, dh), F32)

    st = state_ref[...]
    vn_subs, oi_subs = [], []
    for s in range(ns):
        vn_c, oi_c = [], []
        for c in range(2):
            rows = slice(c * DN_CHUNK, (c + 1) * DN_CHUNK)
            st_b = st.astype(BF16)
            vn = u4[:, s, rows] - _bmm(w4[:, s, rows], st_b)
            oi_c.append(_bmm(qg4[:, s, rows], st_b))
            vn_pad = jnp.concatenate([vn, zeros_half] if c == 0 else [zeros_half, vn], axis=1)
            kg_c = jnp.where(lane_chunk[c], kg_t4[:, s], 0.0).astype(BF16)
            st = st * jnp.exp(tot_b[s][c]) + _bmm(kg_c, vn_pad.astype(BF16))
            vn_c.append(vn)
        vn_subs.append(jnp.concatenate(vn_c, axis=1))
        oi_subs.append(jnp.concatenate(oi_c, axis=1))
    state_ref[...] = st

    vn_all = jnp.stack(vn_subs, axis=1).reshape(gn, PAIR, dh)
    o_inter = jnp.stack(oi_subs, axis=1).reshape(gn, PAIR, dh)
    o = o_inter + _bmm(qk.astype(BF16), vn_all.astype(BF16))
    ms = jnp.mean(o * o, axis=-1, keepdims=True)
    on = (o * lax.rsqrt(ms + NORM_EPS)) * nrm_ref[...]
    for h in range(DN_HEADS):
        for s in range(ns):
            rows = slice(s * PAIR, (s + 1) * PAIR)
            hs = slice(h * dh, (h + 1) * dh)
            zh = z_ref[rows, hs]
            o_ref[rows, hs] = (on[h * ns + s] * (zh * jax.nn.sigmoid(zh))).astype(BF16)


def _delta_call(q, k, v, small, z, prm, nrm, batch, seq):
    t = batch * seq
    tb = min(DELTA_TB, seq)
    nt = seq // tb

    def row_spec(n):
        return pl.BlockSpec((tb, n), lambda b, i: (b * nt + i, 0))

    head_spec = pl.BlockSpec((None, DN_HEADS, tb, DN_HEAD_DIM), lambda b, i: (b, 0, i, 0))
    kern = functools.partial(_delta_kernel, tb=tb)
    return pl.pallas_call(
        kern,
        grid=(batch, nt),
        in_specs=[
            head_spec, head_spec, head_spec, row_spec(LANES),
            row_spec(DN_WIDTH),
            pl.BlockSpec((8, LANES), lambda b, i: (0, 0)),
            pl.BlockSpec((1, DN_HEAD_DIM), lambda b, i: (0, 0)),
        ],
        out_specs=row_spec(DN_WIDTH),
        out_shape=jax.ShapeDtypeStruct((t, DN_WIDTH), BF16),
        scratch_shapes=[pltpu.VMEM((DN_HEADS, DN_HEAD_DIM, DN_HEAD_DIM), F32)],
        compiler_params=_cparams(("parallel", "arbitrary")),
        name="delta_rule",
    )(q, k, v, small, z, prm, nrm)


def _attn_kernel(q_ref, kp_ref, kc_ref, vp_ref, vc_ref, o_ref, lse_ref, *, dilation):
    blk = pl.program_id(2)
    qi = lax.broadcasted_iota(jnp.int32, (DA_BLOCK, DA_BLOCK), 0)
    ki = lax.broadcasted_iota(jnp.int32, (DA_BLOCK, DA_BLOCK), 1)
    valid_prev = (ki >= qi) & (blk > 0)
    valid_cur = ki <= qi
    dist_prev = ((qi - ki + DA_BLOCK) * dilation).astype(F32)
    dist_cur = ((qi - ki) * dilation).astype(F32)
    lane = lax.broadcasted_iota(jnp.int32, (DA_BLOCK, LANES), 1)
    lse_tile = jnp.zeros((DA_BLOCK, LANES), F32)
    scale = DA_HEAD_DIM ** -0.5

    for h in range(DA_HEADS):
        hs = slice(h * DA_HEAD_DIM, (h + 1) * DA_HEAD_DIM)
        slope = 2.0 ** (-ALIBI_MAX_EXP * (h + 1) / DA_HEADS)
        qh = q_ref[:, hs]
        sp = _dot_nt(qh, kp_ref[:, hs]) * scale
        sc = _dot_nt(qh, kc_ref[:, hs]) * scale
        sp = jnp.where(valid_prev, sp + (-slope) * dist_prev, NEG_BIG)
        sc = jnp.where(valid_cur, sc + (-slope) * dist_cur, NEG_BIG)
        mx = jnp.maximum(jnp.max(sp, axis=-1, keepdims=True), jnp.max(sc, axis=-1, keepdims=True))
        pp = jnp.exp(sp - mx)
        pc = jnp.exp(sc - mx)
        l = jnp.sum(pp, axis=-1, keepdims=True) + jnp.sum(pc, axis=-1, keepdims=True)
        acc = _dot(pp.astype(BF16), vp_ref[:, hs]) + _dot(pc.astype(BF16), vc_ref[:, hs])
        o_ref[:, hs] = (acc / l).astype(BF16)
        lse_tile = jnp.where(lane == h, mx + jnp.log(l), lse_tile)
    lse_ref[...] = lse_tile


def _attn_call(q, k, v, batch, seq, dilation):
    r = dilation
    n = seq // r
    nb = n // DA_BLOCK
    w = DA_WIDTH
    qv = q.reshape(batch, n, r * w)
    kv = k.reshape(batch, n, r * w)
    vv = v.reshape(batch, n, r * w)

    cur = pl.BlockSpec((None, DA_BLOCK, w), lambda b, p, i: (b, i, p))
    prev = pl.BlockSpec((None, DA_BLOCK, w), lambda b, p, i: (b, jnp.maximum(i - 1, 0), p))
    kern = functools.partial(_attn_kernel, dilation=r)
    o, lse = pl.pallas_call(
        kern,
        grid=(batch, r, nb),
        in_specs=[cur, prev, cur, prev, cur],
        out_specs=[cur, pl.BlockSpec((None, DA_BLOCK, LANES), lambda b, p, i: (b, i, p))],
        out_shape=[jax.ShapeDtypeStruct((batch, n, r * w), BF16),
                   jax.ShapeDtypeStruct((batch, n, r * LANES), F32)],
        compiler_params=_cparams(("parallel", "parallel", "arbitrary")),
        name=f"dilated_attn_r{r}",
    )(qv, kv, kv, vv, vv)
    return o.reshape(batch * seq, w), lse.reshape(batch * seq, LANES)


def _merge_kernel(h_ref, gt_ref, oa_ref, o1_ref, o2_ref, o3_ref, l1_ref, l2_ref, l3_ref,
                  ga_ref, gb_ref, wa_ref, wb_ref, wo_ref, ex_ref, out_ref):
    l1 = l1_ref[...]
    l2 = l2_ref[...]
    l3 = l3_ref[...]
    mx = jnp.maximum(jnp.maximum(l1, l2), l3)
    e1 = jnp.exp(l1 - mx)
    e2 = jnp.exp(l2 - mx)
    e3 = jnp.exp(l3 - mx)
    den = (e1 + e2) + e3
    ex = ex_ref[...]
    w1 = _mm_exact_rhs(e1 / den, ex)
    w2 = _mm_exact_rhs(e2 / den, ex)
    w3 = _mm_exact_rhs(e3 / den, ex)
    ob = (w1 * o1_ref[...].astype(F32) + w2 * o2_ref[...].astype(F32)) + w3 * o3_ref[...].astype(F32)
    yb = _dot(ob.astype(BF16), wb_ref[...])
    ya = _dot(oa_ref[...], wa_ref[...])
    merged = jax.nn.sigmoid(ga_ref[...]) * ya + jax.nn.sigmoid(gb_ref[...]) * yb
    mo = _dot(merged.astype(BF16), wo_ref[...])
    out_ref[...] = h_ref[...] + gt_ref[...] * mo


def _merge_call(h, mod, oa, outs, lses, ga, gb, wa, wb, wo, expand, layer, seq):
    t, d = h.shape
    tm = min(MERGE_TM, seq)
    tiles_per_batch = seq // tm

    def row_spec(n):
        return pl.BlockSpec((tm, n), lambda i: (i, 0))

    return pl.pallas_call(
        _merge_kernel,
        grid=(t // tm,),
        in_specs=[
            row_spec(d),
            pl.BlockSpec((None, None, None, 1, d), lambda i: (layer, 5, i // tiles_per_batch, 0, 0)),
            row_spec(DN_WIDTH),
            row_spec(DA_WIDTH), row_spec(DA_WIDTH), row_spec(DA_WIDTH),
            row_spec(LANES), row_spec(LANES), row_spec(LANES),
            row_spec(d), row_spec(d),
            pl.BlockSpec((None, DN_WIDTH, d), lambda i: (layer, 0, 0)),
            pl.BlockSpec((None, DA_WIDTH, d), lambda i: (layer, 0, 0)),
            pl.BlockSpec((None, d, d), lambda i: (layer, 0, 0)),
            pl.BlockSpec((LANES, DA_WIDTH), lambda i: (0, 0)),
        ],
        out_specs=row_spec(d),
        out_shape=jax.ShapeDtypeStruct((t, d), F32),
        compiler_params=_cparams(("parallel",)),
        name="merge_out",
    )(h, mod, oa, outs[0], outs[1], outs[2], lses[0], lses[1], lses[2], ga, gb, wa, wb, wo, expand)


def _reorder_w_in(w_in):
    o_small = 4 * DN_WIDTH
    main_a = w_in[:, :, :o_small]
    small = w_in[:, :, o_small:o_small + 2 * DN_HEADS]
    main_b = w_in[:, :, o_small + 2 * DN_HEADS:]
    pad = jnp.zeros(w_in.shape[:2] + (LANES - 2 * DN_HEADS,), w_in.dtype)
    return jnp.concatenate([main_a, main_b, small, pad], axis=-1).astype(BF16)


def kernel(x, c, ada_w, ada_b, ln_ffn1, ln_mix, ln_ffn2, ffn1_wg, ffn1_wu, ffn1_wd, w_in, conv_w,
           a_log, dt_bias, dn_norm, w_a, w_b, w_o, ffn2_wg, ffn2_wu, ffn2_wd, final_norm):
    batch, seq, d = x.shape
    depth = ada_w.shape[0]
    t = batch * seq

    mod = _ada_call(c, ada_w, ada_b)
    mod = mod.reshape(depth, batch, N_ADA, 1, d).transpose(0, 2, 1, 3, 4)

    w_r = _reorder_w_in(w_in)
    bf = lambda a: a.astype(BF16)
    f1g, f1u, f1d = bf(ffn1_wg), bf(ffn1_wu), bf(ffn1_wd)
    f2g, f2u, f2d = bf(ffn2_wg), bf(ffn2_wu), bf(ffn2_wd)
    wa, wb, wo = bf(w_a), bf(w_b), bf(w_o)
    ln1 = ln_ffn1.reshape(depth, 1, d)
    lnm = ln_mix.reshape(depth, 1, d)
    ln2 = ln_ffn2.reshape(depth, 1, d)

    lane = np.arange(LANES)
    expand = jnp.asarray((lane[:, None] == (np.arange(DA_WIDTH)[None, :] // DA_HEAD_DIM)), BF16)
    zpad = jnp.zeros((depth, DN_HEADS), F32)
    prm_rows = jnp.stack([jnp.concatenate([zpad, a_log.astype(F32)], axis=-1),
                          jnp.concatenate([zpad, dt_bias.astype(F32)], axis=-1)], axis=1)
    prm = jnp.pad(prm_rows, ((0, 0), (0, 6), (0, LANES - 2 * DN_HEADS)))

    h = x.reshape(t, d)
    for l in range(depth):
        h = _ffn_call(h, mod, (0, 1, 2), ln1, f1g, f1u, f1d, l, seq)
        q, k, v, z, daq, dak, dav, ga, gb, small = _inproj_call(h, mod, lnm, w_r, conv_w, l, seq)
        oa = _delta_call(q, k, v, small, z, prm[l], dn_norm[l].reshape(1, DN_HEAD_DIM), batch, seq)
        outs, lses = [], []
        for _, dilation in DA_PATTERNS:
            o_p, lse_p = _attn_call(daq, dak, dav, batch, seq, dilation)
            outs.append(o_p)
            lses.append(lse_p)
        h = _merge_call(h, mod, oa, outs, lses, ga, gb, wa, wb, wo, expand, l, seq)
        h = _ffn_call(h, mod, (6, 7, 8), ln2, f2g, f2u, f2d, l, seq,
                      final_norm_w=final_norm if l == depth - 1 else None)
    return h.reshape(batch, seq, d)
```

```python
import functools

import jax
import jax.numpy as jnp
import numpy as np
from jax import lax
from jax.experimental import pallas as pl
from jax.experimental.pallas import tpu as pltpu

F32 = jnp.float32
BF16 = jnp.bfloat16

D_MODEL = 1024
D_FF = 2816
DN_HEADS = 8
DN_HEAD_DIM = 128
DN_WIDTH = DN_HEADS * DN_HEAD_DIM
DN_CONV = 4
DN_CHUNK = 64
CHUNK_SHIFT = 6
DA_HEADS = 12
DA_HEAD_DIM = 64
DA_WIDTH = DA_HEADS * DA_HEAD_DIM
DA_PAIRS = DA_HEADS // 2
DA_PATTERNS = ((128, 1), (512, 4), (2048, 16))
DA_BLOCK = 128
ALIBI_MAX_EXP = 8.0
MACARON_WEIGHT = 0.5
NORM_EPS = 1e-6
N_ADA = 9
LANES = 128

COL_QKV = 0
COL_Z = 3 * DN_WIDTH
COL_DAQ = COL_Z + DN_WIDTH
COL_DAK = COL_DAQ + DA_WIDTH
COL_DAV = COL_DAK + DA_WIDTH
COL_GA = COL_DAV + DA_WIDTH
COL_GB = COL_GA + D_MODEL
COL_SMALL = COL_GB + D_MODEL
IN_COLS_PADDED = COL_SMALL + LANES

VMEM_LIMIT_BYTES = 56 * 1024 * 1024
NEG_BIG = -1e30

FFN_TM = 512
FFN_TF = 1408
INPROJ_TM = 256
DELTA_TB = 256
PAIR = 2 * DN_CHUNK
MERGE_TM = 512
ADA_TN = 1024


def _cparams(sem):
    return pltpu.CompilerParams(dimension_semantics=sem, vmem_limit_bytes=VMEM_LIMIT_BYTES)


def _dot(a, b):
    return jnp.dot(a, b, preferred_element_type=F32)


def _dot_nt(a, b):
    return lax.dot_general(a, b, (((1,), (1,)), ((), ())), preferred_element_type=F32)


def _bmm(a, b):
    return lax.dot_general(a, b, (((2,), (1,)), ((0,), (0,))), preferred_element_type=F32)


def _bmm_nt(a, b):
    return lax.dot_general(a, b, (((2,), (2,)), ((0,), (0,))), preferred_element_type=F32)


def _split2(x):
    hi = x.astype(BF16)
    lo = (x - hi.astype(F32)).astype(BF16)
    return hi, lo


def _split3(x):
    hi = x.astype(BF16)
    r = x - hi.astype(F32)
    mid = r.astype(BF16)
    lo = (r - mid.astype(F32)).astype(BF16)
    return hi, mid, lo


def _mm3(a, b):
    ah, al = _split2(a)
    bh, bl = _split2(b)
    return _dot(ah, bh) + (_dot(ah, bl) + _dot(al, bh))


def _mm_exact_rhs(a, b_bf16):
    hi, mid, lo = _split3(a)
    return _dot(hi, b_bf16) + (_dot(mid, b_bf16) + _dot(lo, b_bf16))


def _rms_modulate(x, ln, sc, sh):
    ms = jnp.mean(x * x, axis=-1, keepdims=True)
    y = (x * lax.rsqrt(ms + NORM_EPS)) * ln
    return y * (1.0 + sc) + sh


def _ada_kernel(c_ref, w_ref, b_ref, o_ref):
    c = c_ref[...]
    ca = c * jax.nn.sigmoid(c)
    o_ref[...] = _mm3(ca, w_ref[...]) + b_ref[...]


def _ada_call(c, ada_w, ada_b):
    depth, d, n = ada_w.shape
    b = c.shape[0]
    return pl.pallas_call(
        _ada_kernel,
        grid=(depth, n // ADA_TN),
        in_specs=[
            pl.BlockSpec((b, d), lambda l, j: (0, 0)),
            pl.BlockSpec((None, d, ADA_TN), lambda l, j: (l, 0, j)),
            pl.BlockSpec((None, 1, ADA_TN), lambda l, j: (l, 0, j)),
        ],
        out_specs=pl.BlockSpec((None, b, ADA_TN), lambda l, j: (l, 0, j)),
        out_shape=jax.ShapeDtypeStruct((depth, b, n), F32),
        compiler_params=_cparams(("arbitrary", "arbitrary")),
        name="ada_mod",
    )(c, ada_w, ada_b.reshape(depth, 1, n))


def _ffn_kernel(x_ref, sh_ref, sc_ref, gt_ref, ln_ref, wg_ref, wu_ref, wd_ref, fn_ref, o_ref,
                xn_ref, acc_ref, *, n_ff_tiles, final_norm):
    j = pl.program_id(1)

    @pl.when(j == 0)
    def _():
        xn_ref[...] = _rms_modulate(x_ref[...], ln_ref[...], sc_ref[...], sh_ref[...]).astype(BF16)
        acc_ref[...] = jnp.zeros_like(acc_ref)

    xn = xn_ref[...]
    g = _dot(xn, wg_ref[...])
    u = _dot(xn, wu_ref[...])
    a = (g * jax.nn.sigmoid(g)) * u
    acc_ref[...] += _dot(a.astype(BF16), wd_ref[...])

    @pl.when(j == n_ff_tiles - 1)
    def _():
        hn = x_ref[...] + (MACARON_WEIGHT * gt_ref[...]) * acc_ref[...]
        if final_norm:
            ms = jnp.mean(hn * hn, axis=-1, keepdims=True)
            hn = (hn * lax.rsqrt(ms + NORM_EPS)) * fn_ref[...]
        o_ref[...] = hn


def _ffn_call(h, mod, ks, ln, wg, wu, wd, layer, seq, final_norm_w=None):
    t, d = h.shape
    tm = min(FFN_TM, seq)
    tiles_per_batch = seq // tm
    n_ff = D_FF // FFN_TF
    final = final_norm_w is not None
    fn = final_norm_w.reshape(1, d) if final else jnp.ones((1, d), F32)

    def mod_spec(k):
        return pl.BlockSpec((None, None, None, 1, d),
                            lambda i, j: (layer, k, i // tiles_per_batch, 0, 0))

    kern = functools.partial(_ffn_kernel, n_ff_tiles=n_ff, final_norm=final)
    return pl.pallas_call(
        kern,
        grid=(t // tm, n_ff),
        in_specs=[
            pl.BlockSpec((tm, d), lambda i, j: (i, 0)),
            mod_spec(ks[0]), mod_spec(ks[1]), mod_spec(ks[2]),
            pl.BlockSpec((None, 1, d), lambda i, j: (layer, 0, 0)),
            pl.BlockSpec((None, d, FFN_TF), lambda i, j: (layer, 0, j)),
            pl.BlockSpec((None, d, FFN_TF), lambda i, j: (layer, 0, j)),
            pl.BlockSpec((None, FFN_TF, d), lambda i, j: (layer, j, 0)),
            pl.BlockSpec((1, d), lambda i, j: (0, 0)),
        ],
        out_specs=pl.BlockSpec((tm, d), lambda i, j: (i, 0)),
        out_shape=jax.ShapeDtypeStruct((t, d), F32),
        scratch_shapes=[pltpu.VMEM((tm, d), BF16), pltpu.VMEM((tm, d), F32)],
        compiler_params=_cparams(("parallel", "arbitrary")),
        name="ffn",
    )(h, mod, mod, mod, ln, wg, wu, wd, fn)


def _inproj_kernel(x_ref, sh_ref, sc_ref, ln_ref, w_ref, cw_ref,
                   q_ref, k_ref, v_ref, z_ref, daq_ref, dak_ref, dav_ref, ga_ref, gb_ref, sm_ref,
                   cbuf_ref, *, tm, tiles_per_batch):
    i = pl.program_id(0)
    halo = 8

    @pl.when(i % tiles_per_batch == 0)
    def _():
        cbuf_ref[0:halo, :] = jnp.zeros((halo, 3 * DN_WIDTH), F32)

    xn = _rms_modulate(x_ref[...], ln_ref[...], sc_ref[...], sh_ref[...]).astype(BF16)
    cbuf_ref[halo:halo + tm, :] = _dot(xn, w_ref[:, COL_QKV:COL_Z])

    for cg in range(3 * DN_HEADS):
        c0 = cg * DN_HEAD_DIM
        cs = slice(c0, c0 + DN_HEAD_DIM)
        y = cbuf_ref[halo - 3:halo - 3 + tm, cs] * cw_ref[0:1, cs]
        for tap in range(1, DN_CONV):
            y = y + cbuf_ref[halo - 3 + tap:halo - 3 + tap + tm, cs] * cw_ref[tap:tap + 1, cs]
        y = y * jax.nn.sigmoid(y)
        head = cg % DN_HEADS
        if cg < 2 * DN_HEADS:
            y = y * lax.rsqrt(jnp.sum(y * y, axis=-1, keepdims=True) + NORM_EPS)
        if cg < DN_HEADS:
            q_ref[head] = (y * (DN_HEAD_DIM ** -0.5)).astype(BF16)
        elif cg < 2 * DN_HEADS:
            k_ref[head] = y.astype(BF16)
        else:
            v_ref[head] = y.astype(BF16)

    cbuf_ref[halo - 3:halo, :] = cbuf_ref[halo + tm - 3:halo + tm, :]

    z_ref[...] = _dot(xn, w_ref[:, COL_Z:COL_DAQ])
    for da_ref, col in ((daq_ref, COL_DAQ), (dak_ref, COL_DAK), (dav_ref, COL_DAV)):
        da = _dot(xn, w_ref[:, col:col + DA_WIDTH]).astype(BF16)
        for p in range(DA_PAIRS):
            da_ref[p] = da[:, p * LANES:(p + 1) * LANES]
    ga_ref[...] = _dot(xn, w_ref[:, COL_GA:COL_GB])
    gb_ref[...] = _dot(xn, w_ref[:, COL_GB:COL_SMALL])
    sm_ref[...] = _dot(xn, w_ref[:, COL_SMALL:IN_COLS_PADDED])


def _inproj_call(h, mod, ln, w_r, conv_w, layer, seq):
    t, d = h.shape
    tm = min(INPROJ_TM, seq)
    tiles_per_batch = seq // tm

    def mod_spec(k):
        return pl.BlockSpec((None, None, None, 1, d), lambda i: (layer, k, i // tiles_per_batch, 0, 0))

    def row_spec(n):
        return pl.BlockSpec((tm, n), lambda i: (i, 0))

    kern = functools.partial(_inproj_kernel, tm=tm, tiles_per_batch=tiles_per_batch)
    outs = [(D_MODEL, F32), (D_MODEL, F32), (LANES, F32)]
    batch = t // seq
    def slab_spec(n_slabs):
        return pl.BlockSpec((None, n_slabs, tm, LANES),
                            lambda i: (i // tiles_per_batch, 0, i % tiles_per_batch, 0))

    def slab_shape(n_slabs):
        return jax.ShapeDtypeStruct((batch, n_slabs, seq, LANES), BF16)

    return pl.pallas_call(
        kern,
        grid=(t // tm,),
        in_specs=[
            row_spec(d), mod_spec(3), mod_spec(4),
            pl.BlockSpec((None, 1, d), lambda i: (layer, 0, 0)),
            pl.BlockSpec((None, d, IN_COLS_PADDED), lambda i: (layer, 0, 0),
                         pipeline_mode=pl.Buffered(1)),
            pl.BlockSpec((None, DN_CONV, 3 * DN_WIDTH), lambda i: (layer, 0, 0)),
        ],
        out_specs=([slab_spec(DN_HEADS)] * 3 + [row_spec(DN_WIDTH)] + [slab_spec(DA_PAIRS)] * 3
                   + [row_spec(n) for n, _ in outs]),
        out_shape=([slab_shape(DN_HEADS)] * 3 + [jax.ShapeDtypeStruct((t, DN_WIDTH), F32)]
                   + [slab_shape(DA_PAIRS)] * 3 + [jax.ShapeDtypeStruct((t, n), dt) for n, dt in outs]),
        scratch_shapes=[pltpu.VMEM((tm + 8, 3 * DN_WIDTH), F32)],
        compiler_params=_cparams(("arbitrary",)),
        name="inproj",
    )(h, mod, mod, ln, w_r, conv_w)


def _delta_kernel(q_ref, k_ref, v_ref, sm_ref, z_ref, prm_ref, nrm_ref, o_ref, state_ref, *, tb):
    ns = tb // PAIR
    gn = DN_HEADS * ns
    dh = DN_HEAD_DIM
    t_idx = pl.program_id(1)

    @pl.when(t_idx == 0)
    def _():
        state_ref[...] = jnp.zeros_like(state_ref)

    ri = lax.broadcasted_iota(jnp.int32, (PAIR, PAIR), 0)
    ci = lax.broadcasted_iota(jnp.int32, (PAIR, PAIR), 1)
    r_chunk = ri >> CHUNK_SHIFT
    c_chunk = ci >> CHUNK_SHIFT
    same_chunk = r_chunk == c_chunk
    incl = same_chunk & (ci <= ri)
    strict = same_chunk & (ci < ri)
    one = jnp.ones((PAIR, PAIR), F32)
    zero = jnp.zeros((PAIR, PAIR), F32)
    lower_incl = jnp.where(incl, one, zero).astype(BF16)
    upper_strict = jnp.where(same_chunk & (ci > ri), one, zero).astype(BF16)
    upper_incl = jnp.where(same_chunk & (ri <= ci), one, zero).astype(BF16)
    chunk_sel = [jnp.where(r_chunk == c, one, zero).astype(BF16) for c in range(2)]

    sm = sm_ref[...]
    beta_full = jax.nn.sigmoid(sm)
    xs = sm + prm_ref[1:2, :]
    softplus = jnp.maximum(xs, 0.0) + jnp.log1p(jnp.exp(-jnp.abs(xs)))
    g_full = (-jnp.exp(prm_ref[0:1, :])) * softplus

    beta_b, gcc_b, rem_b, gcr_b = [], [], [], []
    tot_b = [[None, None] for _ in range(ns)]
    per_sub = []
    for s in range(ns):
        rows = slice(s * PAIR, (s + 1) * PAIR)
        g_s = g_full[rows]
        gh, gm, gl = _split3(g_s)
        gc_col = _dot(lower_incl, gh) + (_dot(lower_incl, gm) + _dot(lower_incl, gl))
        rem_col = _dot(upper_strict, gh) + (_dot(upper_strict, gm) + _dot(upper_strict, gl))
        g_t = g_s.T
        gc_row = _mm_exact_rhs(g_t, upper_incl)
        tots = [_mm_exact_rhs(g_t, chunk_sel[c]) for c in range(2)]
        per_sub.append((beta_full[rows], gc_col, rem_col, gc_row, tots))
    for h in range(DN_HEADS):
        gl_ = DN_HEADS + h
        for s in range(ns):
            beta_s, gc_col, rem_col, gc_row, tots = per_sub[s]
            beta_b.append(jnp.broadcast_to(beta_s[:, h:h + 1], (PAIR, dh)))
            gcc_b.append(jnp.broadcast_to(gc_col[:, gl_:gl_ + 1], (PAIR, PAIR)))
            rem_b.append(jnp.broadcast_to(rem_col[:, gl_:gl_ + 1], (PAIR, dh)))
            gcr_b.append(jnp.broadcast_to(gc_row[gl_:gl_ + 1, :], (PAIR, PAIR)))
    for s in range(ns):
        for c in range(2):
            tot_b[s][c] = jnp.stack(
                [jnp.broadcast_to(per_sub[s][4][c][DN_HEADS + h:DN_HEADS + h + 1, :], (dh, dh))
                 for h in range(DN_HEADS)], axis=0)
    beta_b = jnp.stack(beta_b, axis=0)
    gcc_b = jnp.stack(gcc_b, axis=0)
    rem_b = jnp.stack(rem_b, axis=0)
    gcr_b = jnp.stack(gcr_b, axis=0)

    q3 = q_ref[...].reshape(gn, PAIR, dh)
    k3 = k_ref[...].reshape(gn, PAIR, dh)
    qf = q3.astype(F32)
    kf = k3.astype(F32)
    vf = v_ref[...].reshape(gn, PAIR, dh).astype(F32)
    kb = kf * beta_b
    vb = vf * beta_b
    dec = jnp.exp(jnp.where(incl[None], gcc_b - gcr_b, NEG_BIG))
    m = jnp.where(strict[None], _bmm_nt(kb.astype(BF16), k3) * dec, 0.0)
    qk = _bmm_nt(q3, k3) * dec

    y = -jnp.where((((ri ^ ci) == 1) & ((ri & 1) == 1))[None], m, 0.0)
    for level in range(1, CHUNK_SHIFT):
        s_ = 1 << level
        blk = ((ri >> (level + 1)) == (ci >> (level + 1))) & ((ri & s_) != 0) & ((ci & s_) == 0)
        c_off = jnp.where(blk[None], m, 0.0)
        p = c_off + _bmm(c_off.astype(BF16), y.astype(BF16))
        y = (y - p) - _bmm(y.astype(BF16), p.astype(BF16))

    egc = jnp.exp(gcc_b)
    rhs = jnp.concatenate([vb, kb * egc], axis=-1)
    uw = rhs + _bmm(y.astype(BF16), rhs.astype(BF16))
    u4 = uw[:, :, :dh].reshape(DN_HEADS, ns, PAIR, dh)
    w4 = uw[:, :, dh:].astype(BF16).reshape(DN_HEADS, ns, PAIR, dh)
    qg4 = (qf * egc).astype(BF16).reshape(DN_HEADS, ns, PAIR, dh)
    kg_t4 = jnp.swapaxes(kf * jnp.exp(rem_b), 1, 2).reshape(DN_HEADS, ns, dh, PAIR)
    lane_chunk = [(c_chunk == c)[None] for c in range(2)]
    zeros_half = jnp.zeros((DN_HEADS, DN_CHUNK, dh), F32)

    st = state_ref[...]
    vn_subs, oi_subs = [], []
    for s in range(ns):
        vn_c, oi_c = [], []
        for c in range(2):
            rows = slice(c * DN_CHUNK, (c + 1) * DN_CHUNK)
            st_b = st.astype(BF16)
            vn = u4[:, s, rows] - _bmm(w4[:, s, rows], st_b)
            oi_c.append(_bmm(qg4[:, s, rows], st_b))
            vn_pad = jnp.concatenate([vn, zeros_half] if c == 0 else [zeros_half, vn], axis=1)
            kg_c = jnp.where(lane_chunk[c], kg_t4[:, s], 0.0).astype(BF16)
            st = st * jnp.exp(tot_b[s][c]) + _bmm(kg_c, vn_pad.astype(BF16))
            vn_c.append(vn)
        vn_subs.append(jnp.concatenate(vn_c, axis=1))
        oi_subs.append(jnp.concatenate(oi_c, axis=1))
    state_ref[...] = st

    vn_all = jnp.stack(vn_subs, axis=1).reshape(gn, PAIR, dh)
    o_inter = jnp.stack(oi_subs, axis=1).reshape(gn, PAIR, dh)
    o = o_inter + _bmm(qk.astype(BF16), vn_all.astype(BF16))
    ms = jnp.mean(o * o, axis=-1, keepdims=True)
    on = (o * lax.rsqrt(ms + NORM_EPS)) * nrm_ref[...]
    for h in range(DN_HEADS):
        for s in range(ns):
            rows = slice(s * PAIR, (s + 1) * PAIR)
            hs = slice(h * dh, (h + 1) * dh)
            zh = z_ref[rows, hs]
            o_ref[rows, hs] = (on[h * ns + s] * (zh * jax.nn.sigmoid(zh))).astype(BF16)


def _delta_call(q, k, v, small, z, prm, nrm, batch, seq):
    t = batch * seq
    tb = min(DELTA_TB, seq)
    nt = seq // tb

    def row_spec(n):
        return pl.BlockSpec((tb, n), lambda b, i: (b * nt + i, 0))

    head_spec = pl.BlockSpec((None, DN_HEADS, tb, DN_HEAD_DIM), lambda b, i: (b, 0, i, 0))
    kern = functools.partial(_delta_kernel, tb=tb)
    return pl.pallas_call(
        kern,
        grid=(batch, nt),
        in_specs=[
            head_spec, head_spec, head_spec, row_spec(LANES),
            row_spec(DN_WIDTH),
            pl.BlockSpec((8, LANES), lambda b, i: (0, 0)),
            pl.BlockSpec((1, DN_HEAD_DIM), lambda b, i: (0, 0)),
        ],
        out_specs=row_spec(DN_WIDTH),
        out_shape=jax.ShapeDtypeStruct((t, DN_WIDTH), BF16),
        scratch_shapes=[pltpu.VMEM((DN_HEADS, DN_HEAD_DIM, DN_HEAD_DIM), F32)],
        compiler_params=_cparams(("parallel", "arbitrary")),
        name="delta_rule",
    )(q, k, v, small, z, prm, nrm)


def _attn_kernel(q_ref, kp_ref, kc_ref, vp_ref, vc_ref, bias_ref, o_ref, lse_ref):
    lane = lax.broadcasted_iota(jnp.int32, (DA_PAIRS, DA_BLOCK, LANES), 2)
    low = lane < DA_HEAD_DIM
    q = q_ref[...] * (DA_HEAD_DIM ** -0.5)
    zero = jnp.zeros_like(q)
    qq = jnp.concatenate([jnp.where(low, q, zero), jnp.where(low, zero, q)], axis=1)
    k2 = jnp.concatenate([kp_ref[...], kc_ref[...]], axis=1)
    v2 = jnp.concatenate([vp_ref[...], vc_ref[...]], axis=1)
    v2 = jnp.concatenate([v2, jnp.ones_like(v2)], axis=2)

    s = _bmm_nt(qq, k2) + bias_ref[...]
    mx = jnp.max(s, axis=-1, keepdims=True)
    p = jnp.exp(s - mx).astype(BF16)
    acc = _bmm(p, v2)
    den = acc[:, :, LANES:]
    o = acc[:, :, :LANES] / den
    o_ref[...] = jnp.where(low, o[:, :DA_BLOCK], o[:, DA_BLOCK:]).astype(BF16)

    lse = mx + jnp.log(den)
    lane2 = lax.broadcasted_iota(jnp.int32, (DA_BLOCK, LANES), 1)
    lse_tile = jnp.zeros((DA_BLOCK, LANES), F32)
    for h in range(DA_HEADS):
        rows = slice((h % 2) * DA_BLOCK, (h % 2 + 1) * DA_BLOCK)
        lse_tile = jnp.where(lane2 == h, lse[h // 2, rows], lse_tile)
    lse_ref[...] = lse_tile


def _attn_bias_tables(dilation):
    qi = np.arange(DA_BLOCK)[:, None]
    ki = np.arange(DA_BLOCK)[None, :]
    dist = np.concatenate([qi - ki + DA_BLOCK, qi - ki], axis=1)
    valid_other = np.concatenate([ki >= qi, ki <= qi], axis=1)
    valid_first = np.concatenate([np.zeros((DA_BLOCK, DA_BLOCK), bool), ki <= qi], axis=1)
    slopes = np.float32(2.0) ** (-np.float32(ALIBI_MAX_EXP) * np.arange(1, DA_HEADS + 1, dtype=np.float32)
                                 / np.float32(DA_HEADS))
    pen = -slopes[:, None, None] * (dist * dilation).astype(np.float32)[None]
    tables = []
    for valid in (valid_first, valid_other):
        t = np.where(valid[None], pen, np.float32(NEG_BIG)).astype(np.float32)
        tables.append(t.reshape(DA_PAIRS, 2 * DA_BLOCK, 2 * DA_BLOCK))
    return jnp.asarray(np.stack(tables, axis=0))


def _attn_call(q, k, v, batch, seq, dilation):
    r = dilation
    n = seq // r
    nb = n // DA_BLOCK
    qv = q.reshape(batch, DA_PAIRS, n, r * LANES)
    kv = k.reshape(batch, DA_PAIRS, n, r * LANES)
    vv = v.reshape(batch, DA_PAIRS, n, r * LANES)
    bias = _attn_bias_tables(r)

    blk = (None, DA_PAIRS, DA_BLOCK, LANES)
    cur = pl.BlockSpec(blk, lambda b, p, i: (b, 0, i, p))
    prev = pl.BlockSpec(blk, lambda b, p, i: (b, 0, jnp.maximum(i - 1, 0), p))
    o, lse = pl.pallas_call(
        _attn_kernel,
        grid=(batch, r, nb),
        in_specs=[cur, prev, cur, prev, cur,
                  pl.BlockSpec((None, DA_PAIRS, 2 * DA_BLOCK, 2 * DA_BLOCK),
                               lambda b, p, i: (jnp.minimum(i, 1), 0, 0, 0))],
        out_specs=[cur, pl.BlockSpec((None, DA_BLOCK, LANES), lambda b, p, i: (b, i, p))],
        out_shape=[jax.ShapeDtypeStruct((batch, DA_PAIRS, n, r * LANES), BF16),
                   jax.ShapeDtypeStruct((batch, n, r * LANES), F32)],
        compiler_params=_cparams(("parallel", "parallel", "arbitrary")),
        name=f"dilated_attn_r{r}",
    )(qv, kv, kv, vv, vv, bias)
    return o.reshape(batch, DA_PAIRS, seq, LANES), lse.reshape(batch * seq, LANES)


def _merge_kernel(h_ref, gt_ref, oa_ref, o1_ref, o2_ref, o3_ref, l1_ref, l2_ref, l3_ref,
                  ga_ref, gb_ref, wa_ref, wb_ref, wo_ref, ex_ref, out_ref):
    l1 = l1_ref[...]
    l2 = l2_ref[...]
    l3 = l3_ref[...]
    mx = jnp.maximum(jnp.maximum(l1, l2), l3)
    e1 = jnp.exp(l1 - mx)
    e2 = jnp.exp(l2 - mx)
    e3 = jnp.exp(l3 - mx)
    den = (e1 + e2) + e3
    ex = ex_ref[...]
    w1 = _mm_exact_rhs(e1 / den, ex)
    w2 = _mm_exact_rhs(e2 / den, ex)
    w3 = _mm_exact_rhs(e3 / den, ex)
    ob = []
    for p in range(DA_PAIRS):
        ls = slice(p * LANES, (p + 1) * LANES)
        ob.append(((w1[:, ls] * o1_ref[p].astype(F32) + w2[:, ls] * o2_ref[p].astype(F32))
                   + w3[:, ls] * o3_ref[p].astype(F32)).astype(BF16))
    yb = _dot(jnp.concatenate(ob, axis=-1), wb_ref[...])
    ya = _dot(oa_ref[...], wa_ref[...])
    merged = jax.nn.sigmoid(ga_ref[...]) * ya + jax.nn.sigmoid(gb_ref[...]) * yb
    mo = _dot(merged.astype(BF16), wo_ref[...])
    out_ref[...] = h_ref[...] + gt_ref[...] * mo


def _merge_call(h, mod, oa, outs, lses, ga, gb, wa, wb, wo, expand, layer, seq):
    t, d = h.shape
    tm = min(MERGE_TM, seq)
    tiles_per_batch = seq // tm

    def row_spec(n):
        return pl.BlockSpec((tm, n), lambda i: (i, 0))

    pair_spec = pl.BlockSpec((None, DA_PAIRS, tm, LANES),
                             lambda i: (i // tiles_per_batch, 0, i % tiles_per_batch, 0))
    return pl.pallas_call(
        _merge_kernel,
        grid=(t // tm,),
        in_specs=[
            row_spec(d),
            pl.BlockSpec((None, None, None, 1, d), lambda i: (layer, 5, i // tiles_per_batch, 0, 0)),
            row_spec(DN_WIDTH),
            pair_spec, pair_spec, pair_spec,
            row_spec(LANES), row_spec(LANES), row_spec(LANES),
            row_spec(d), row_spec(d),
            pl.BlockSpec((None, DN_WIDTH, d), lambda i: (layer, 0, 0)),
            pl.BlockSpec((None, DA_WIDTH, d), lambda i: (layer, 0, 0)),
            pl.BlockSpec((None, d, d), lambda i: (layer, 0, 0)),
            pl.BlockSpec((LANES, DA_WIDTH), lambda i: (0, 0)),
        ],
        out_specs=row_spec(d),
        out_shape=jax.ShapeDtypeStruct((t, d), F32),
        compiler_params=_cparams(("parallel",)),
        name="merge_out",
    )(h, mod, oa, outs[0], outs[1], outs[2], lses[0], lses[1], lses[2], ga, gb, wa, wb, wo, expand)


def _reorder_w_in(w_in):
    o_small = 4 * DN_WIDTH
    main_a = w_in[:, :, :o_small]
    small = w_in[:, :, o_small:o_small + 2 * DN_HEADS]
    main_b = w_in[:, :, o_small + 2 * DN_HEADS:]
    pad = jnp.zeros(w_in.shape[:2] + (LANES - 2 * DN_HEADS,), w_in.dtype)
    return jnp.concatenate([main_a, main_b, small, pad], axis=-1).astype(BF16)


def kernel(x, c, ada_w, ada_b, ln_ffn1, ln_mix, ln_ffn2, ffn1_wg, ffn1_wu, ffn1_wd, w_in, conv_w,
           a_log, dt_bias, dn_norm, w_a, w_b, w_o, ffn2_wg, ffn2_wu, ffn2_wd, final_norm):
    batch, seq, d = x.shape
    depth = ada_w.shape[0]
    t = batch * seq

    mod = _ada_call(c, ada_w, ada_b)
    mod = mod.reshape(depth, batch, N_ADA, 1, d).transpose(0, 2, 1, 3, 4)

    w_r = _reorder_w_in(w_in)
    bf = lambda a: a.astype(BF16)
    f1g, f1u, f1d = bf(ffn1_wg), bf(ffn1_wu), bf(ffn1_wd)
    f2g, f2u, f2d = bf(ffn2_wg), bf(ffn2_wu), bf(ffn2_wd)
    wa, wb, wo = bf(w_a), bf(w_b), bf(w_o)
    ln1 = ln_ffn1.reshape(depth, 1, d)
    lnm = ln_mix.reshape(depth, 1, d)
    ln2 = ln_ffn2.reshape(depth, 1, d)

    lane = np.arange(LANES)
    expand = jnp.asarray((lane[:, None] == (np.arange(DA_WIDTH)[None, :] // DA_HEAD_DIM)), BF16)
    zpad = jnp.zeros((depth, DN_HEADS), F32)
    prm_rows = jnp.stack([jnp.concatenate([zpad, a_log.astype(F32)], axis=-1),
                          jnp.concatenate([zpad, dt_bias.astype(F32)], axis=-1)], axis=1)
    prm = jnp.pad(prm_rows, ((0, 0), (0, 6), (0, LANES - 2 * DN_HEADS)))

    h = x.reshape(t, d)
    for l in range(depth):
        h = _ffn_call(h, mod, (0, 1, 2), ln1, f1g, f1u, f1d, l, seq)
        q, k, v, z, daq, dak, dav, ga, gb, small = _inproj_call(h, mod, lnm, w_r, conv_w, l, seq)
        oa = _delta_call(q, k, v, small, z, prm[l], dn_norm[l].reshape(1, DN_HEAD_DIM), batch, seq)
        outs, lses = [], []
        for _, dilation in DA_PATTERNS:
            o_p, lse_p = _attn_call(daq, dak, dav, batch, seq, dilation)
            outs.append(o_p)
            lses.append(lse_p)
        h = _merge_call(h, mod, oa, outs, lses, ga, gb, wa, wb, wo, expand, l, seq)
        h = _ffn_call(h, mod, (6, 7, 8), ln2, f2g, f2u, f2d, l, seq,
                      final_norm_w=final_norm if l == depth - 1 else None)
    return h.reshape(batch, seq, d)
```

```python
import functools

import jax
import jax.numpy as jnp
import numpy as np
from jax import lax
from jax.experimental import pallas as pl
from jax.experimental.pallas import tpu as pltpu

F32 = jnp.float32
BF16 = jnp.bfloat16

D_MODEL = 1024
D_FF = 2816
DN_HEADS = 8
DN_HEAD_DIM = 128
DN_WIDTH = DN_HEADS * DN_HEAD_DIM
DN_CONV = 4
DN_CHUNK = 64
CHUNK_SHIFT = 6
DA_HEADS = 12
DA_HEAD_DIM = 64
DA_WIDTH = DA_HEADS * DA_HEAD_DIM
DA_PAIRS = DA_HEADS // 2
DA_PATTERNS = ((128, 1), (512, 4), (2048, 16))
DA_BLOCK = 128
ALIBI_MAX_EXP = 8.0
MACARON_WEIGHT = 0.5
NORM_EPS = 1e-6
N_ADA = 9
LANES = 128

COL_QKV = 0
COL_Z = 3 * DN_WIDTH
COL_DAQ = COL_Z + DN_WIDTH
COL_DAK = COL_DAQ + DA_WIDTH
COL_DAV = COL_DAK + DA_WIDTH
COL_GA = COL_DAV + DA_WIDTH
COL_GB = COL_GA + D_MODEL
COL_SMALL = COL_GB + D_MODEL
IN_COLS_PADDED = COL_SMALL + LANES

VMEM_LIMIT_BYTES = 56 * 1024 * 1024
NEG_BIG = -1e30

FFN_TM = 1024
FFN_TF = 256
INPROJ_TM = 512
DELTA_TB = 256
PAIR = 2 * DN_CHUNK
MERGE_TM = 512
ADA_TN = 1024


def _cparams(sem):
    return pltpu.CompilerParams(dimension_semantics=sem, vmem_limit_bytes=VMEM_LIMIT_BYTES)


def _dot(a, b):
    return jnp.dot(a, b, preferred_element_type=F32)


def _dot_nt(a, b):
    return lax.dot_general(a, b, (((1,), (1,)), ((), ())), preferred_element_type=F32)


def _bmm(a, b):
    return lax.dot_general(a, b, (((2,), (1,)), ((0,), (0,))), preferred_element_type=F32)


def _bmm_nt(a, b):
    return lax.dot_general(a, b, (((2,), (2,)), ((0,), (0,))), preferred_element_type=F32)


def _split2(x):
    hi = x.astype(BF16)
    lo = (x - hi.astype(F32)).astype(BF16)
    return hi, lo


def _split3(x):
    hi = x.astype(BF16)
    r = x - hi.astype(F32)
    mid = r.astype(BF16)
    lo = (r - mid.astype(F32)).astype(BF16)
    return hi, mid, lo


def _mm3(a, b):
    ah, al = _split2(a)
    bh, bl = _split2(b)
    return _dot(ah, bh) + (_dot(ah, bl) + _dot(al, bh))


def _mm_exact_rhs(a, b_bf16):
    hi, mid, lo = _split3(a)
    return _dot(hi, b_bf16) + (_dot(mid, b_bf16) + _dot(lo, b_bf16))


def _rms_modulate(x, ln, sc, sh):
    ms = jnp.mean(x * x, axis=-1, keepdims=True)
    y = (x * lax.rsqrt(ms + NORM_EPS)) * ln
    return y * (1.0 + sc) + sh


def _ada_kernel(c_ref, w_ref, b_ref, o_ref):
    c = c_ref[...]
    ca = c * jax.nn.sigmoid(c)
    o_ref[...] = _mm3(ca, w_ref[...]) + b_ref[...]


def _ada_call(c, ada_w, ada_b):
    depth, d, n = ada_w.shape
    b = c.shape[0]
    return pl.pallas_call(
        _ada_kernel,
        grid=(depth, n // ADA_TN),
        in_specs=[
            pl.BlockSpec((b, d), lambda l, j: (0, 0)),
            pl.BlockSpec((None, d, ADA_TN), lambda l, j: (l, 0, j)),
            pl.BlockSpec((None, 1, ADA_TN), lambda l, j: (l, 0, j)),
        ],
        out_specs=pl.BlockSpec((None, b, ADA_TN), lambda l, j: (l, 0, j)),
        out_shape=jax.ShapeDtypeStruct((depth, b, n), F32),
        compiler_params=_cparams(("arbitrary", "arbitrary")),
        name="ada_mod",
    )(c, ada_w, ada_b.reshape(depth, 1, n))


def _ffn_kernel(x_ref, sh_ref, sc_ref, gt_ref, ln_ref, wg_ref, wu_ref, wd_ref, fn_ref, o_ref,
                *, final_norm):
    x = x_ref[...]
    xn = _rms_modulate(x, ln_ref[...], sc_ref[...], sh_ref[...]).astype(BF16)
    acc = None
    for c0 in range(0, D_FF, FFN_TF):
        cs = slice(c0, c0 + FFN_TF)
        g = _dot(xn, wg_ref[:, cs])
        u = _dot(xn, wu_ref[:, cs])
        a = (g * jax.nn.sigmoid(g)) * u
        part = _dot(a.astype(BF16), wd_ref[cs, :])
        acc = part if acc is None else acc + part
    hn = x + (MACARON_WEIGHT * gt_ref[...]) * acc
    if final_norm:
        ms = jnp.mean(hn * hn, axis=-1, keepdims=True)
        hn = (hn * lax.rsqrt(ms + NORM_EPS)) * fn_ref[...]
    o_ref[...] = hn


def _ffn_call(h, mod, ks, ln, wg, wu, wd, layer, seq, final_norm_w=None):
    t, d = h.shape
    tm = min(FFN_TM, seq)
    tiles_per_batch = seq // tm
    final = final_norm_w is not None
    fn = final_norm_w.reshape(1, d) if final else jnp.ones((1, d), F32)

    def mod_spec(k):
        return pl.BlockSpec((None, None, None, 1, d), lambda i: (layer, k, i // tiles_per_batch, 0, 0))

    def weight_spec(rows, cols):
        return pl.BlockSpec((None, rows, cols), lambda i: (layer, 0, 0), pipeline_mode=pl.Buffered(1))

    kern = functools.partial(_ffn_kernel, final_norm=final)
    return pl.pallas_call(
        kern,
        grid=(t // tm,),
        in_specs=[
            pl.BlockSpec((tm, d), lambda i: (i, 0)),
            mod_spec(ks[0]), mod_spec(ks[1]), mod_spec(ks[2]),
            pl.BlockSpec((None, 1, d), lambda i: (layer, 0, 0)),
            weight_spec(d, D_FF), weight_spec(d, D_FF), weight_spec(D_FF, d),
            pl.BlockSpec((1, d), lambda i: (0, 0)),
        ],
        out_specs=pl.BlockSpec((tm, d), lambda i: (i, 0)),
        out_shape=jax.ShapeDtypeStruct((t, d), F32),
        compiler_params=_cparams(("parallel",)),
        name="ffn",
    )(h, mod, mod, mod, ln, wg, wu, wd, fn)


def _inproj_kernel(x_ref, sh_ref, sc_ref, ln_ref, w_ref, cw_ref,
                   q_ref, k_ref, v_ref, z_ref, daq_ref, dak_ref, dav_ref, ga_ref, gb_ref, sm_ref,
                   cbuf_ref, xn_ref, *, tm, tiles_per_batch):
    i = pl.program_id(0)
    halo = 8

    @pl.when(i % tiles_per_batch == 0)
    def _():
        cbuf_ref[0:halo, :] = jnp.zeros((halo, 3 * DN_WIDTH), F32)

    xn_ref[...] = _rms_modulate(x_ref[...], ln_ref[...], sc_ref[...], sh_ref[...]).astype(BF16)
    cbuf_ref[halo:halo + tm, :] = _dot(xn_ref[...], w_ref[:, COL_QKV:COL_Z])

    def plain(ref, col, n):
        def run(c0):
            ref[:, c0:c0 + n] = _dot(xn_ref[...], w_ref[:, col + c0:col + c0 + n]).astype(ref.dtype)
        return [functools.partial(run, c0) for c0 in range(0, ref.shape[-1], n)]

    def paired(ref, col):
        def run(p0):
            da = _dot(xn_ref[...], w_ref[:, col + p0 * LANES:col + (p0 + 2) * LANES]).astype(BF16)
            ref[p0] = da[:, :LANES]
            ref[p0 + 1] = da[:, LANES:]
        return [functools.partial(run, p0) for p0 in range(0, DA_PAIRS, 2)]

    others = (plain(z_ref, COL_Z, 2 * LANES) + paired(daq_ref, COL_DAQ) + paired(dak_ref, COL_DAK)
              + paired(dav_ref, COL_DAV) + plain(ga_ref, COL_GA, 2 * LANES)
              + plain(gb_ref, COL_GB, 2 * LANES) + plain(sm_ref, COL_SMALL, LANES))

    for cg in range(3 * DN_HEADS):
        c0 = cg * DN_HEAD_DIM
        cs = slice(c0, c0 + DN_HEAD_DIM)
        y = cbuf_ref[halo:halo + tm, cs] * cw_ref[DN_CONV - 1:DN_CONV, cs]
        for tap in range(DN_CONV - 2, -1, -1):
            r0 = halo - (DN_CONV - 1) + tap
            y = y + cbuf_ref[r0:r0 + tm, cs] * cw_ref[tap:tap + 1, cs]
        y = y * jax.nn.sigmoid(y)
        head = cg % DN_HEADS
        if cg < 2 * DN_HEADS:
            y = y * lax.rsqrt(jnp.sum(y * y, axis=-1, keepdims=True) + NORM_EPS)
        if cg < DN_HEADS:
            q_ref[head] = (y * (DN_HEAD_DIM ** -0.5)).astype(BF16)
        elif cg < 2 * DN_HEADS:
            k_ref[head] = y.astype(BF16)
        else:
            v_ref[head] = y.astype(BF16)
        if others:
            others.pop(0)()

    cbuf_ref[halo - 3:halo, :] = cbuf_ref[halo + tm - 3:halo + tm, :]
    for run in others:
        run()


def _inproj_call(h, mod, ln, w_r, conv_w, layer, seq):
    t, d = h.shape
    tm = min(INPROJ_TM, seq)
    tiles_per_batch = seq // tm

    def mod_spec(k):
        return pl.BlockSpec((None, None, None, 1, d), lambda i: (layer, k, i // tiles_per_batch, 0, 0))

    def row_spec(n):
        return pl.BlockSpec((tm, n), lambda i: (i, 0))

    kern = functools.partial(_inproj_kernel, tm=tm, tiles_per_batch=tiles_per_batch)
    outs = [(D_MODEL, BF16), (D_MODEL, BF16), (LANES, F32)]
    batch = t // seq
    def slab_spec(n_slabs):
        return pl.BlockSpec((None, n_slabs, tm, LANES),
                            lambda i: (i // tiles_per_batch, 0, i % tiles_per_batch, 0))

    def slab_shape(n_slabs):
        return jax.ShapeDtypeStruct((batch, n_slabs, seq, LANES), BF16)

    return pl.pallas_call(
        kern,
        grid=(t // tm,),
        in_specs=[
            row_spec(d), mod_spec(3), mod_spec(4),
            pl.BlockSpec((None, 1, d), lambda i: (layer, 0, 0)),
            pl.BlockSpec((None, d, IN_COLS_PADDED), lambda i: (layer, 0, 0),
                         pipeline_mode=pl.Buffered(1)),
            pl.BlockSpec((None, DN_CONV, 3 * DN_WIDTH), lambda i: (layer, 0, 0)),
        ],
        out_specs=([slab_spec(DN_HEADS)] * 3 + [row_spec(DN_WIDTH)] + [slab_spec(DA_PAIRS)] * 3
                   + [row_spec(n) for n, _ in outs]),
        out_shape=([slab_shape(DN_HEADS)] * 3 + [jax.ShapeDtypeStruct((t, DN_WIDTH), BF16)]
                   + [slab_shape(DA_PAIRS)] * 3 + [jax.ShapeDtypeStruct((t, n), dt) for n, dt in outs]),
        scratch_shapes=[pltpu.VMEM((tm + 8, 3 * DN_WIDTH), F32), pltpu.VMEM((tm, d), BF16)],
        compiler_params=_cparams(("arbitrary",)),
        name="inproj",
    )(h, mod, mod, ln, w_r, conv_w)


def _delta_kernel(q_ref, k_ref, v_ref, sm_ref, z_ref, prm_ref, nrm_ref, o_ref, state_ref, *, tb):
    ns = tb // PAIR
    gn = DN_HEADS * ns
    dh = DN_HEAD_DIM
    t_idx = pl.program_id(1)

    @pl.when(t_idx == 0)
    def _():
        state_ref[...] = jnp.zeros_like(state_ref)

    ri = lax.broadcasted_iota(jnp.int32, (PAIR, PAIR), 0)
    ci = lax.broadcasted_iota(jnp.int32, (PAIR, PAIR), 1)
    r_chunk = ri >> CHUNK_SHIFT
    c_chunk = ci >> CHUNK_SHIFT
    same_chunk = r_chunk == c_chunk
    incl = same_chunk & (ci <= ri)
    strict = same_chunk & (ci < ri)
    one = jnp.ones((PAIR, PAIR), F32)
    zero = jnp.zeros((PAIR, PAIR), F32)
    lower_incl = jnp.where(incl, one, zero).astype(BF16)
    upper_strict = jnp.where(same_chunk & (ci > ri), one, zero).astype(BF16)
    upper_incl = jnp.where(same_chunk & (ri <= ci), one, zero).astype(BF16)
    chunk_sel = [jnp.where(r_chunk == c, one, zero).astype(BF16) for c in range(2)]

    sm = sm_ref[...]
    beta_full = jax.nn.sigmoid(sm)
    xs = sm + prm_ref[1:2, :]
    softplus = jnp.maximum(xs, 0.0) + jnp.log1p(jnp.exp(-jnp.abs(xs)))
    g_full = (-jnp.exp(prm_ref[0:1, :])) * softplus

    beta_b, gcc_b, rem_b, gcr_b = [], [], [], []
    tot_b = [[None, None] for _ in range(ns)]
    per_sub = []
    for s in range(ns):
        rows = slice(s * PAIR, (s + 1) * PAIR)
        g_s = g_full[rows]
        gh, gm, gl = _split3(g_s)
        gc_col = _dot(lower_incl, gh) + (_dot(lower_incl, gm) + _dot(lower_incl, gl))
        rem_col = _dot(upper_strict, gh) + (_dot(upper_strict, gm) + _dot(upper_strict, gl))
        g_t = g_s.T
        gc_row = _mm_exact_rhs(g_t, upper_incl)
        tots = [_mm_exact_rhs(g_t, chunk_sel[c]) for c in range(2)]
        per_sub.append((beta_full[rows], gc_col, rem_col, gc_row, tots))
    for h in range(DN_HEADS):
        gl_ = DN_HEADS + h
        for s in range(ns):
            beta_s, gc_col, rem_col, gc_row, tots = per_sub[s]
            beta_b.append(jnp.broadcast_to(beta_s[:, h:h + 1], (PAIR, dh)))
            gcc_b.append(jnp.broadcast_to(gc_col[:, gl_:gl_ + 1], (PAIR, PAIR)))
            rem_b.append(jnp.broadcast_to(rem_col[:, gl_:gl_ + 1], (PAIR, dh)))
            gcr_b.append(jnp.broadcast_to(gc_row[gl_:gl_ + 1, :], (PAIR, PAIR)))
    for s in range(ns):
        for c in range(2):
            tot_b[s][c] = jnp.stack(
                [jnp.broadcast_to(per_sub[s][4][c][DN_HEADS + h:DN_HEADS + h + 1, :], (dh, dh))
                 for h in range(DN_HEADS)], axis=0)
    beta_b = jnp.stack(beta_b, axis=0)
    gcc_b = jnp.stack(gcc_b, axis=0)
    rem_b = jnp.stack(rem_b, axis=0)
    gcr_b = jnp.stack(gcr_b, axis=0)

    q3 = q_ref[...].reshape(gn, PAIR, dh)
    k3 = k_ref[...].reshape(gn, PAIR, dh)
    qf = q3.astype(F32)
    kf = k3.astype(F32)
    vf = v_ref[...].reshape(gn, PAIR, dh).astype(F32)
    kb = kf * beta_b
    vb = vf * beta_b
    dec = jnp.exp(jnp.where(incl[None], gcc_b - gcr_b, NEG_BIG))
    m = jnp.where(strict[None], _bmm_nt(kb.astype(BF16), k3) * dec, 0.0)
    qk = _bmm_nt(q3, k3) * dec

    y = -jnp.where((((ri ^ ci) == 1) & ((ri & 1) == 1))[None], m, 0.0)
    for level in range(1, CHUNK_SHIFT):
        s_ = 1 << level
        blk = ((ri >> (level + 1)) == (ci >> (level + 1))) & ((ri & s_) != 0) & ((ci & s_) == 0)
        c_off = jnp.where(blk[None], m, 0.0)
        p = c_off + _bmm(c_off.astype(BF16), y.astype(BF16))
        y = (y - p) - _bmm(y.astype(BF16), p.astype(BF16))

    egc = jnp.exp(gcc_b)
    rhs = jnp.concatenate([vb, kb * egc], axis=-1)
    uw = rhs + _bmm(y.astype(BF16), rhs.astype(BF16))
    u4 = uw[:, :, :dh].reshape(DN_HEADS, ns, PAIR, dh)
    w4 = uw[:, :, dh:].astype(BF16).reshape(DN_HEADS, ns, PAIR, dh)
    qg4 = (qf * egc).astype(BF16).reshape(DN_HEADS, ns, PAIR, dh)
    kg_t4 = jnp.swapaxes(kf * jnp.exp(rem_b), 1, 2).reshape(DN_HEADS, ns, dh, PAIR)
    lane_chunk = [(c_chunk == c)[None] for c in range(2)]
    zeros_half = jnp.zeros((DN_HEADS, DN_CHUNK, dh), F32)

    st = state_ref[...]
    vn_subs, oi_subs = [], []
    for s in range(ns):
        vn_c, oi_c = [], []
        for c in range(2):
            rows = slice(c * DN_CHUNK, (c + 1) * DN_CHUNK)
            st_b = st.astype(BF16)
            vn = u4[:, s, rows] - _bmm(w4[:, s, rows], st_b)
            oi_c.append(_bmm(qg4[:, s, rows], st_b))
            vn_pad = jnp.concatenate([vn, zeros_half] if c == 0 else [zeros_half, vn], axis=1)
            kg_c = jnp.where(lane_chunk[c], kg_t4[:, s], 0.0).astype(BF16)
            st = st * jnp.exp(tot_b[s][c]) + _bmm(kg_c, vn_pad.astype(BF16))
            vn_c.append(vn)
        vn_subs.append(jnp.concatenate(vn_c, axis=1))
        oi_subs.append(jnp.concatenate(oi_c, axis=1))
    state_ref[...] = st

    vn_all = jnp.stack(vn_subs, axis=1).reshape(gn, PAIR, dh)
    o_inter = jnp.stack(oi_subs, axis=1).reshape(gn, PAIR, dh)
    o = o_inter + _bmm(qk.astype(BF16), vn_all.astype(BF16))
    ms = jnp.mean(o * o, axis=-1, keepdims=True)
    on = (o * lax.rsqrt(ms + NORM_EPS)) * nrm_ref[...]
    for h in range(DN_HEADS):
        for s in range(ns):
            rows = slice(s * PAIR, (s + 1) * PAIR)
            hs = slice(h * dh, (h + 1) * dh)
            zh = z_ref[rows, hs].astype(F32)
            o_ref[rows, hs] = (on[h * ns + s] * (zh * jax.nn.sigmoid(zh))).astype(BF16)


def _delta_call(q, k, v, small, z, prm, nrm, batch, seq):
    t = batch * seq
    tb = min(DELTA_TB, seq)
    nt = seq // tb

    def row_spec(n):
        return pl.BlockSpec((tb, n), lambda b, i: (b * nt + i, 0))

    head_spec = pl.BlockSpec((None, DN_HEADS, tb, DN_HEAD_DIM), lambda b, i: (b, 0, i, 0))
    kern = functools.partial(_delta_kernel, tb=tb)
    return pl.pallas_call(
        kern,
        grid=(batch, nt),
        in_specs=[
            head_spec, head_spec, head_spec, row_spec(LANES),
            row_spec(DN_WIDTH),
            pl.BlockSpec((8, LANES), lambda b, i: (0, 0)),
            pl.BlockSpec((1, DN_HEAD_DIM), lambda b, i: (0, 0)),
        ],
        out_specs=row_spec(DN_WIDTH),
        out_shape=jax.ShapeDtypeStruct((t, DN_WIDTH), BF16),
        scratch_shapes=[pltpu.VMEM((DN_HEADS, DN_HEAD_DIM, DN_HEAD_DIM), F32)],
        compiler_params=_cparams(("parallel", "arbitrary")),
        name="delta_rule",
    )(q, k, v, small, z, prm, nrm)


def _attn_kernel(q_ref, kp_ref, kc_ref, vp_ref, vc_ref, bias_ref, o_ref, lse_ref):
    lane = lax.broadcasted_iota(jnp.int32, (DA_PAIRS, DA_BLOCK, LANES), 2)
    low = lane < DA_HEAD_DIM
    q = q_ref[...] * (DA_HEAD_DIM ** -0.5)
    zero = jnp.zeros_like(q)
    qq = jnp.concatenate([jnp.where(low, q, zero), jnp.where(low, zero, q)], axis=1)
    k2 = jnp.concatenate([kp_ref[...], kc_ref[...]], axis=1)
    v2 = jnp.concatenate([vp_ref[...], vc_ref[...]], axis=1)
    v2 = jnp.concatenate([v2, jnp.ones_like(v2)], axis=2)

    s = _bmm_nt(qq, k2) + bias_ref[...]
    mx = jnp.max(s, axis=-1, keepdims=True)
    p = jnp.exp(s - mx).astype(BF16)
    acc = _bmm(p, v2)
    den = acc[:, :, LANES:]
    o = acc[:, :, :LANES] / den
    o_ref[...] = jnp.where(low, o[:, :DA_BLOCK], o[:, DA_BLOCK:]).astype(BF16)

    lse = mx + jnp.log(den)
    lane2 = lax.broadcasted_iota(jnp.int32, (DA_BLOCK, LANES), 1)
    lse_tile = jnp.zeros((DA_BLOCK, LANES), F32)
    for h in range(DA_HEADS):
        rows = slice((h % 2) * DA_BLOCK, (h % 2 + 1) * DA_BLOCK)
        lse_tile = jnp.where(lane2 == h, lse[h // 2, rows], lse_tile)
    lse_ref[...] = lse_tile


def _attn_bias_tables(dilation):
    qi = np.arange(DA_BLOCK)[:, None]
    ki = np.arange(DA_BLOCK)[None, :]
    dist = np.concatenate([qi - ki + DA_BLOCK, qi - ki], axis=1)
    valid_other = np.concatenate([ki >= qi, ki <= qi], axis=1)
    valid_first = np.concatenate([np.zeros((DA_BLOCK, DA_BLOCK), bool), ki <= qi], axis=1)
    slopes = np.float32(2.0) ** (-np.float32(ALIBI_MAX_EXP) * np.arange(1, DA_HEADS + 1, dtype=np.float32)
                                 / np.float32(DA_HEADS))
    pen = -slopes[:, None, None] * (dist * dilation).astype(np.float32)[None]
    tables = []
    for valid in (valid_first, valid_other):
        t = np.where(valid[None], pen, np.float32(NEG_BIG)).astype(np.float32)
        tables.append(t.reshape(DA_PAIRS, 2 * DA_BLOCK, 2 * DA_BLOCK))
    return jnp.asarray(np.stack(tables, axis=0))


def _attn_call(q, k, v, batch, seq, dilation):
    r = dilation
    n = seq // r
    nb = n // DA_BLOCK
    qv = q.reshape(batch, DA_PAIRS, n, r * LANES)
    kv = k.reshape(batch, DA_PAIRS, n, r * LANES)
    vv = v.reshape(batch, DA_PAIRS, n, r * LANES)
    bias = _attn_bias_tables(r)

    blk = (None, DA_PAIRS, DA_BLOCK, LANES)
    cur = pl.BlockSpec(blk, lambda b, p, i: (b, 0, i, p))
    prev = pl.BlockSpec(blk, lambda b, p, i: (b, 0, jnp.maximum(i - 1, 0), p))
    o, lse = pl.pallas_call(
        _attn_kernel,
        grid=(batch, r, nb),
        in_specs=[cur, prev, cur, prev, cur,
                  pl.BlockSpec((None, DA_PAIRS, 2 * DA_BLOCK, 2 * DA_BLOCK),
                               lambda b, p, i: (jnp.minimum(i, 1), 0, 0, 0))],
        out_specs=[cur, pl.BlockSpec((None, DA_BLOCK, LANES), lambda b, p, i: (b, i, p))],
        out_shape=[jax.ShapeDtypeStruct((batch, DA_PAIRS, n, r * LANES), BF16),
                   jax.ShapeDtypeStruct((batch, n, r * LANES), F32)],
        compiler_params=_cparams(("parallel", "parallel", "arbitrary")),
        name=f"dilated_attn_r{r}",
    )(qv, kv, kv, vv, vv, bias)
    return o.reshape(batch, DA_PAIRS, seq, LANES), lse.reshape(batch * seq, LANES)


def _merge_kernel(h_ref, gt_ref, oa_ref, o1_ref, o2_ref, o3_ref, l1_ref, l2_ref, l3_ref,
                  ga_ref, gb_ref, wa_ref, wb_ref, wo_ref, ex_ref, out_ref):
    l1 = l1_ref[...]
    l2 = l2_ref[...]
    l3 = l3_ref[...]
    mx = jnp.maximum(jnp.maximum(l1, l2), l3)
    e1 = jnp.exp(l1 - mx)
    e2 = jnp.exp(l2 - mx)
    e3 = jnp.exp(l3 - mx)
    den = (e1 + e2) + e3
    ex = ex_ref[...]
    w1 = _mm_exact_rhs(e1 / den, ex)
    w2 = _mm_exact_rhs(e2 / den, ex)
    w3 = _mm_exact_rhs(e3 / den, ex)
    ob = []
    for p in range(DA_PAIRS):
        ls = slice(p * LANES, (p + 1) * LANES)
        ob.append(((w1[:, ls] * o1_ref[p].astype(F32) + w2[:, ls] * o2_ref[p].astype(F32))
                   + w3[:, ls] * o3_ref[p].astype(F32)).astype(BF16))
    yb = _dot(jnp.concatenate(ob, axis=-1), wb_ref[...])
    ya = _dot(oa_ref[...], wa_ref[...])
    merged = (jax.nn.sigmoid(ga_ref[...].astype(F32)) * ya
              + jax.nn.sigmoid(gb_ref[...].astype(F32)) * yb)
    mo = _dot(merged.astype(BF16), wo_ref[...])
    out_ref[...] = h_ref[...] + gt_ref[...] * mo


def _merge_call(h, mod, oa, outs, lses, ga, gb, wa, wb, wo, expand, layer, seq):
    t, d = h.shape
    tm = min(MERGE_TM, seq)
    tiles_per_batch = seq // tm

    def row_spec(n):
        return pl.BlockSpec((tm, n), lambda i: (i, 0))

    pair_spec = pl.BlockSpec((None, DA_PAIRS, tm, LANES),
                             lambda i: (i // tiles_per_batch, 0, i % tiles_per_batch, 0))
    return pl.pallas_call(
        _merge_kernel,
        grid=(t // tm,),
        in_specs=[
            row_spec(d),
            pl.BlockSpec((None, None, None, 1, d), lambda i: (layer, 5, i // tiles_per_batch, 0, 0)),
            row_spec(DN_WIDTH),
            pair_spec, pair_spec, pair_spec,
            row_spec(LANES), row_spec(LANES), row_spec(LANES),
            row_spec(d), row_spec(d),
            pl.BlockSpec((None, DN_WIDTH, d), lambda i: (layer, 0, 0)),
            pl.BlockSpec((None, DA_WIDTH, d), lambda i: (layer, 0, 0)),
            pl.BlockSpec((None, d, d), lambda i: (layer, 0, 0)),
            pl.BlockSpec((LANES, DA_WIDTH), lambda i: (0, 0)),
        ],
        out_specs=row_spec(d),
        out_shape=jax.ShapeDtypeStruct((t, d), F32),
        compiler_params=_cparams(("parallel",)),
        name="merge_out",
    )(h, mod, oa, outs[0], outs[1], outs[2], lses[0], lses[1], lses[2], ga, gb, wa, wb, wo, expand)


def _reorder_w_in(w_in):
    o_small = 4 * DN_WIDTH
    main_a = w_in[:, :, :o_small]
    small = w_in[:, :, o_small:o_small + 2 * DN_HEADS]
    main_b = w_in[:, :, o_small + 2 * DN_HEADS:]
    pad = jnp.zeros(w_in.shape[:2] + (LANES - 2 * DN_HEADS,), w_in.dtype)
    return jnp.concatenate([main_a, main_b, small, pad], axis=-1).astype(BF16)


def kernel(x, c, ada_w, ada_b, ln_ffn1, ln_mix, ln_ffn2, ffn1_wg, ffn1_wu, ffn1_wd, w_in, conv_w,
           a_log, dt_bias, dn_norm, w_a, w_b, w_o, ffn2_wg, ffn2_wu, ffn2_wd, final_norm):
    batch, seq, d = x.shape
    depth = ada_w.shape[0]
    t = batch * seq

    mod = _ada_call(c, ada_w, ada_b)
    mod = mod.reshape(depth, batch, N_ADA, 1, d).transpose(0, 2, 1, 3, 4)

    w_r = _reorder_w_in(w_in)
    bf = lambda a: a.astype(BF16)
    f1g, f1u, f1d = bf(ffn1_wg), bf(ffn1_wu), bf(ffn1_wd)
    f2g, f2u, f2d = bf(ffn2_wg), bf(ffn2_wu), bf(ffn2_wd)
    wa, wb, wo = bf(w_a), bf(w_b), bf(w_o)
    ln1 = ln_ffn1.reshape(depth, 1, d)
    lnm = ln_mix.reshape(depth, 1, d)
    ln2 = ln_ffn2.reshape(depth, 1, d)

    lane = np.arange(LANES)
    expand = jnp.asarray((lane[:, None] == (np.arange(DA_WIDTH)[None, :] // DA_HEAD_DIM)), BF16)
    zpad = jnp.zeros((depth, DN_HEADS), F32)
    prm_rows = jnp.stack([jnp.concatenate([zpad, a_log.astype(F32)], axis=-1),
                          jnp.concatenate([zpad, dt_bias.astype(F32)], axis=-1)], axis=1)
    prm = jnp.pad(prm_rows, ((0, 0), (0, 6), (0, LANES - 2 * DN_HEADS)))

    h = x.reshape(t, d)
    for l in range(depth):
        h = _ffn_call(h, mod, (0, 1, 2), ln1, f1g, f1u, f1d, l, seq)
        q, k, v, z, daq, dak, dav, ga, gb, small = _inproj_call(h, mod, lnm, w_r, conv_w, l, seq)
        oa = _delta_call(q, k, v, small, z, prm[l], dn_norm[l].reshape(1, DN_HEAD_DIM), batch, seq)
        outs, lses = [], []
        for _, dilation in DA_PATTERNS:
            o_p, lse_p = _attn_call(daq, dak, dav, batch, seq, dilation)
            outs.append(o_p)
            lses.append(lse_p)
        h = _merge_call(h, mod, oa, outs, lses, ga, gb, wa, wb, wo, expand, l, seq)
        h = _ffn_call(h, mod, (6, 7, 8), ln2, f2g, f2u, f2d, l, seq,
                      final_norm_w=final_norm if l == depth - 1 else None)
    return h.reshape(batch, seq, d)
```

```python
import functools

import jax
import jax.numpy as jnp
import numpy as np
from jax import lax
from jax.experimental import pallas as pl
from jax.experimental.pallas import tpu as pltpu

F32 = jnp.float32
BF16 = jnp.bfloat16

D_MODEL = 1024
D_FF = 2816
DN_HEADS = 8
DN_HEAD_DIM = 128
DN_WIDTH = DN_HEADS * DN_HEAD_DIM
DN_CONV = 4
DN_CHUNK = 64
CHUNK_SHIFT = 6
DA_HEADS = 12
DA_HEAD_DIM = 64
DA_WIDTH = DA_HEADS * DA_HEAD_DIM
DA_PAIRS = DA_HEADS // 2
DA_PATTERNS = ((128, 1), (512, 4), (2048, 16))
DA_BLOCK = 128
ALIBI_MAX_EXP = 8.0
MACARON_WEIGHT = 0.5
NORM_EPS = 1e-6
N_ADA = 9
LANES = 128

COL_QKV = 0
COL_Z = 3 * DN_WIDTH
COL_DAQ = COL_Z + DN_WIDTH
COL_DAK = COL_DAQ + DA_WIDTH
COL_DAV = COL_DAK + DA_WIDTH
COL_GA = COL_DAV + DA_WIDTH
COL_GB = COL_GA + D_MODEL
COL_SMALL = COL_GB + D_MODEL
IN_COLS_PADDED = COL_SMALL + LANES

VMEM_LIMIT_BYTES = 56 * 1024 * 1024
NEG_BIG = -1e30

FFN_TM = 1024
FFN_TF = 256
INPROJ_TM = 512
DELTA_TB = 256
PAIR = 2 * DN_CHUNK
MERGE_TM = 512
ADA_TN = 1024


def _cparams(sem):
    return pltpu.CompilerParams(dimension_semantics=sem, vmem_limit_bytes=VMEM_LIMIT_BYTES)


DR_MOD = 16
DR_SUB = 4


def _residue_shape(batch, seq, slabs, dtype):
    lead = (batch, DR_SUB, DR_SUB) + ((slabs,) if slabs else ())
    return jax.ShapeDtypeStruct(lead + (seq // DR_MOD, LANES), dtype)


def _residue_block(tm, tiles_per_batch, slabs):
    if slabs:
        return pl.BlockSpec((None, DR_SUB, DR_SUB, slabs, tm // DR_MOD, LANES),
                            lambda i: (i // tiles_per_batch, 0, 0, 0, i % tiles_per_batch, 0))
    return pl.BlockSpec((None, DR_SUB, DR_SUB, tm // DR_MOD, LANES),
                        lambda i: (i // tiles_per_batch, 0, 0, i % tiles_per_batch, 0))


def _dot(a, b):
    return jnp.dot(a, b, preferred_element_type=F32)


def _dot_nt(a, b):
    return lax.dot_general(a, b, (((1,), (1,)), ((), ())), preferred_element_type=F32)


def _bmm(a, b):
    return lax.dot_general(a, b, (((2,), (1,)), ((0,), (0,))), preferred_element_type=F32)


def _bmm_nt(a, b):
    return lax.dot_general(a, b, (((2,), (2,)), ((0,), (0,))), preferred_element_type=F32)


def _split2(x):
    hi = x.astype(BF16)
    lo = (x - hi.astype(F32)).astype(BF16)
    return hi, lo


def _split3(x):
    hi = x.astype(BF16)
    r = x - hi.astype(F32)
    mid = r.astype(BF16)
    lo = (r - mid.astype(F32)).astype(BF16)
    return hi, mid, lo


def _mm3(a, b):
    ah, al = _split2(a)
    bh, bl = _split2(b)
    return _dot(ah, bh) + (_dot(ah, bl) + _dot(al, bh))


def _mm_exact_rhs(a, b_bf16):
    hi, mid, lo = _split3(a)
    return _dot(hi, b_bf16) + (_dot(mid, b_bf16) + _dot(lo, b_bf16))


def _mm_split2_rhs(a, b_bf16):
    hi, lo = _split2(a)
    return _dot(hi, b_bf16) + _dot(lo, b_bf16)


def _rms_modulate(x, ln, sc, sh):
    ms = jnp.mean(x * x, axis=-1, keepdims=True)
    y = (x * lax.rsqrt(ms + NORM_EPS)) * ln
    return y * (1.0 + sc) + sh


def _ada_kernel(c_ref, w_ref, b_ref, o_ref):
    c = c_ref[...]
    ca = c * jax.nn.sigmoid(c)
    o_ref[...] = _mm3(ca, w_ref[...]) + b_ref[...]


def _ada_call(c, ada_w, ada_b):
    depth, d, n = ada_w.shape
    b = c.shape[0]
    return pl.pallas_call(
        _ada_kernel,
        grid=(depth, n // ADA_TN),
        in_specs=[
            pl.BlockSpec((b, d), lambda l, j: (0, 0)),
            pl.BlockSpec((None, d, ADA_TN), lambda l, j: (l, 0, j)),
            pl.BlockSpec((None, 1, ADA_TN), lambda l, j: (l, 0, j)),
        ],
        out_specs=pl.BlockSpec((None, b, ADA_TN), lambda l, j: (l, 0, j)),
        out_shape=jax.ShapeDtypeStruct((depth, b, n), F32),
        compiler_params=_cparams(("arbitrary", "arbitrary")),
        name="ada_mod",
    )(c, ada_w, ada_b.reshape(depth, 1, n))


def _ffn_kernel(x_ref, sh_ref, sc_ref, gt_ref, ln_ref, wg_ref, wu_ref, wd_ref, fn_ref, o_ref,
                *, final_norm):
    x = x_ref[...]
    xn = _rms_modulate(x, ln_ref[...], sc_ref[...], sh_ref[...]).astype(BF16)
    acc = None
    for c0 in range(0, D_FF, FFN_TF):
        cs = slice(c0, c0 + FFN_TF)
        g = _dot(xn, wg_ref[:, cs])
        u = _dot(xn, wu_ref[:, cs])
        a = (g * jax.nn.sigmoid(g)) * u
        part = _dot(a.astype(BF16), wd_ref[cs, :])
        acc = part if acc is None else acc + part
    hn = x + (MACARON_WEIGHT * gt_ref[...]) * acc
    if final_norm:
        ms = jnp.mean(hn * hn, axis=-1, keepdims=True)
        hn = (hn * lax.rsqrt(ms + NORM_EPS)) * fn_ref[...]
    o_ref[...] = hn


def _ffn_call(h, mod, ks, ln, wg, wu, wd, layer, seq, final_norm_w=None):
    t, d = h.shape
    tm = min(FFN_TM, seq)
    tiles_per_batch = seq // tm
    final = final_norm_w is not None
    fn = final_norm_w.reshape(1, d) if final else jnp.ones((1, d), F32)

    def mod_spec(k):
        return pl.BlockSpec((None, None, None, 1, d), lambda i: (layer, k, i // tiles_per_batch, 0, 0))

    def weight_spec(rows, cols):
        return pl.BlockSpec((None, rows, cols), lambda i: (layer, 0, 0), pipeline_mode=pl.Buffered(1))

    kern = functools.partial(_ffn_kernel, final_norm=final)
    return pl.pallas_call(
        kern,
        grid=(t // tm,),
        in_specs=[
            pl.BlockSpec((tm, d), lambda i: (i, 0)),
            mod_spec(ks[0]), mod_spec(ks[1]), mod_spec(ks[2]),
            pl.BlockSpec((None, 1, d), lambda i: (layer, 0, 0)),
            weight_spec(d, D_FF), weight_spec(d, D_FF), weight_spec(D_FF, d),
            pl.BlockSpec((1, d), lambda i: (0, 0)),
        ],
        out_specs=pl.BlockSpec((tm, d), lambda i: (i, 0)),
        out_shape=jax.ShapeDtypeStruct((t, d), F32),
        compiler_params=_cparams(("parallel",)),
        name="ffn",
    )(h, mod, mod, mod, ln, wg, wu, wd, fn)


def _inproj_kernel(x_ref, sh_ref, sc_ref, ln_ref, w_ref, cw_ref,
                   q_ref, k_ref, v_ref, z_ref, daq_ref, dak_ref, dav_ref, ga_ref, gb_ref, sm_ref,
                   daq_res_ref, dak_res_ref, dav_res_ref,
                   cbuf_ref, xn_ref, stage_ref, *, tm, tiles_per_batch):
    i = pl.program_id(0)
    halo = 8

    @pl.when(i % tiles_per_batch == 0)
    def _():
        cbuf_ref[0:halo, :] = jnp.zeros((halo, 3 * DN_WIDTH), F32)

    xn_ref[...] = _rms_modulate(x_ref[...], ln_ref[...], sc_ref[...], sh_ref[...]).astype(BF16)
    cbuf_ref[halo:halo + tm, :] = _dot(xn_ref[...], w_ref[:, COL_QKV:COL_Z])

    def plain(ref, col, n):
        def run(c0):
            ref[:, c0:c0 + n] = _dot(xn_ref[...], w_ref[:, col + c0:col + c0 + n]).astype(ref.dtype)
        return [functools.partial(run, c0) for c0 in range(0, ref.shape[-1], n)]

    def paired(ref, res_ref, col):
        def run(p0):
            da = _dot(xn_ref[...], w_ref[:, col + p0 * LANES:col + (p0 + 2) * LANES])
            for half in range(2):
                slab = da[:, half * LANES:(half + 1) * LANES]
                ref[p0 + half] = slab.astype(BF16)
                stage_ref[half] = slab
                for p16 in range(DR_MOD):
                    rows = stage_ref[half, pl.ds(p16, tm // DR_MOD, stride=DR_MOD), :]
                    res_ref[p16 % DR_SUB, p16 // DR_SUB, p0 + half] = rows.astype(BF16)
        return [functools.partial(run, p0) for p0 in range(0, DA_PAIRS, 2)]

    others = (plain(z_ref, COL_Z, 2 * LANES) + paired(daq_ref, daq_res_ref, COL_DAQ)
              + paired(dak_ref, dak_res_ref, COL_DAK) + paired(dav_ref, dav_res_ref, COL_DAV)
              + plain(ga_ref, COL_GA, 2 * LANES)
              + plain(gb_ref, COL_GB, 2 * LANES) + plain(sm_ref, COL_SMALL, LANES))

    for cg in range(3 * DN_HEADS):
        c0 = cg * DN_HEAD_DIM
        cs = slice(c0, c0 + DN_HEAD_DIM)
        y = cbuf_ref[halo:halo + tm, cs] * cw_ref[DN_CONV - 1:DN_CONV, cs]
        for tap in range(DN_CONV - 2, -1, -1):
            r0 = halo - (DN_CONV - 1) + tap
            y = y + cbuf_ref[r0:r0 + tm, cs] * cw_ref[tap:tap + 1, cs]
        y = y * jax.nn.sigmoid(y)
        head = cg % DN_HEADS
        if cg < 2 * DN_HEADS:
            y = y * lax.rsqrt(jnp.sum(y * y, axis=-1, keepdims=True) + NORM_EPS)
        if cg < DN_HEADS:
            q_ref[head] = (y * (DN_HEAD_DIM ** -0.5)).astype(BF16)
        elif cg < 2 * DN_HEADS:
            k_ref[head] = y.astype(BF16)
        else:
            v_ref[head] = y.astype(BF16)
        if others:
            others.pop(0)()

    cbuf_ref[halo - 3:halo, :] = cbuf_ref[halo + tm - 3:halo + tm, :]
    for run in others:
        run()


def _inproj_call(h, mod, ln, w_r, conv_w, layer, seq):
    t, d = h.shape
    tm = min(INPROJ_TM, seq)
    tiles_per_batch = seq // tm

    def mod_spec(k):
        return pl.BlockSpec((None, None, None, 1, d), lambda i: (layer, k, i // tiles_per_batch, 0, 0))

    def row_spec(n):
        return pl.BlockSpec((tm, n), lambda i: (i, 0))

    kern = functools.partial(_inproj_kernel, tm=tm, tiles_per_batch=tiles_per_batch)
    outs = [(D_MODEL, BF16), (D_MODEL, BF16), (LANES, F32)]
    batch = t // seq
    def slab_spec(n_slabs):
        return pl.BlockSpec((None, n_slabs, tm, LANES),
                            lambda i: (i // tiles_per_batch, 0, i % tiles_per_batch, 0))

    def slab_shape(n_slabs):
        return jax.ShapeDtypeStruct((batch, n_slabs, seq, LANES), BF16)

    return pl.pallas_call(
        kern,
        grid=(t // tm,),
        in_specs=[
            row_spec(d), mod_spec(3), mod_spec(4),
            pl.BlockSpec((None, 1, d), lambda i: (layer, 0, 0)),
            pl.BlockSpec((None, d, IN_COLS_PADDED), lambda i: (layer, 0, 0),
                         pipeline_mode=pl.Buffered(1)),
            pl.BlockSpec((None, DN_CONV, 3 * DN_WIDTH), lambda i: (layer, 0, 0)),
        ],
        out_specs=([slab_spec(DN_HEADS)] * 3 + [row_spec(DN_WIDTH)] + [slab_spec(DA_PAIRS)] * 3
                   + [row_spec(n) for n, _ in outs]
                   + [_residue_block(tm, tiles_per_batch, DA_PAIRS)] * 3),
        out_shape=([slab_shape(DN_HEADS)] * 3 + [jax.ShapeDtypeStruct((t, DN_WIDTH), BF16)]
                   + [slab_shape(DA_PAIRS)] * 3 + [jax.ShapeDtypeStruct((t, n), dt) for n, dt in outs]
                   + [_residue_shape(batch, seq, DA_PAIRS, BF16)] * 3),
        scratch_shapes=[pltpu.VMEM((tm + 8, 3 * DN_WIDTH), F32), pltpu.VMEM((tm, d), BF16),
                        pltpu.VMEM((2, tm, LANES), F32)],
        compiler_params=_cparams(("arbitrary",)),
        name="inproj",
    )(h, mod, mod, ln, w_r, conv_w)


def _delta_kernel(q_ref, k_ref, v_ref, sm_ref, z_ref, prm_ref, nrm_ref, o_ref, state_ref, *, tb):
    ns = tb // PAIR
    gn = DN_HEADS * ns
    dh = DN_HEAD_DIM
    t_idx = pl.program_id(1)

    @pl.when(t_idx == 0)
    def _():
        state_ref[...] = jnp.zeros_like(state_ref)

    ri = lax.broadcasted_iota(jnp.int32, (PAIR, PAIR), 0)
    ci = lax.broadcasted_iota(jnp.int32, (PAIR, PAIR), 1)
    r_chunk = ri >> CHUNK_SHIFT
    c_chunk = ci >> CHUNK_SHIFT
    same_chunk = r_chunk == c_chunk
    incl = same_chunk & (ci <= ri)
    strict = same_chunk & (ci < ri)
    one = jnp.ones((PAIR, PAIR), F32)
    zero = jnp.zeros((PAIR, PAIR), F32)
    lower_incl = jnp.where(incl, one, zero).astype(BF16)
    upper_strict = jnp.where(same_chunk & (ci > ri), one, zero).astype(BF16)
    upper_incl = jnp.where(same_chunk & (ri <= ci), one, zero).astype(BF16)
    chunk_sel = [jnp.where(r_chunk == c, one, zero).astype(BF16) for c in range(2)]

    sm = sm_ref[...]
    beta_full = jax.nn.sigmoid(sm)
    xs = sm + prm_ref[1:2, :]
    softplus = jnp.maximum(xs, 0.0) + jnp.log1p(jnp.exp(-jnp.abs(xs)))
    g_full = (-jnp.exp(prm_ref[0:1, :])) * softplus

    beta_b, gcc_b, rem_b, gcr_b = [], [], [], []
    tot_b = [[None, None] for _ in range(ns)]
    per_sub = []
    for s in range(ns):
        rows = slice(s * PAIR, (s + 1) * PAIR)
        g_s = g_full[rows]
        gh, gm, gl = _split3(g_s)
        gc_col = _dot(lower_incl, gh) + (_dot(lower_incl, gm) + _dot(lower_incl, gl))
        rem_col = _dot(upper_strict, gh) + (_dot(upper_strict, gm) + _dot(upper_strict, gl))
        g_t = g_s.T
        gc_row = _mm_exact_rhs(g_t, upper_incl)
        tots = [_mm_exact_rhs(g_t, chunk_sel[c]) for c in range(2)]
        per_sub.append((beta_full[rows], gc_col, rem_col, gc_row, tots))
    for h in range(DN_HEADS):
        gl_ = DN_HEADS + h
        for s in range(ns):
            beta_s, gc_col, rem_col, gc_row, tots = per_sub[s]
            beta_b.append(jnp.broadcast_to(beta_s[:, h:h + 1], (PAIR, dh)))
            gcc_b.append(jnp.broadcast_to(gc_col[:, gl_:gl_ + 1], (PAIR, PAIR)))
            rem_b.append(jnp.broadcast_to(rem_col[:, gl_:gl_ + 1], (PAIR, dh)))
            gcr_b.append(jnp.broadcast_to(gc_row[gl_:gl_ + 1, :], (PAIR, PAIR)))
    for s in range(ns):
        for c in range(2):
            tot_b[s][c] = jnp.stack(
                [jnp.broadcast_to(per_sub[s][4][c][DN_HEADS + h:DN_HEADS + h + 1, :], (dh, dh))
                 for h in range(DN_HEADS)], axis=0)
    beta_b = jnp.stack(beta_b, axis=0)
    gcc_b = jnp.stack(gcc_b, axis=0)
    rem_b = jnp.stack(rem_b, axis=0)
    gcr_b = jnp.stack(gcr_b, axis=0)

    q3 = q_ref[...].reshape(gn, PAIR, dh)
    k3 = k_ref[...].reshape(gn, PAIR, dh)
    qf = q3.astype(F32)
    kf = k3.astype(F32)
    vf = v_ref[...].reshape(gn, PAIR, dh).astype(F32)
    kb = kf * beta_b
    vb = vf * beta_b
    dec = jnp.exp(jnp.where(incl[None], gcc_b - gcr_b, NEG_BIG))
    m = jnp.where(strict[None], _bmm_nt(kb.astype(BF16), k3) * dec, 0.0)
    qk = _bmm_nt(q3, k3) * dec

    y = -jnp.where((((ri ^ ci) == 1) & ((ri & 1) == 1))[None], m, 0.0)
    for level in range(1, CHUNK_SHIFT):
        s_ = 1 << level
        blk = ((ri >> (level + 1)) == (ci >> (level + 1))) & ((ri & s_) != 0) & ((ci & s_) == 0)
        c_off = jnp.where(blk[None], m, 0.0)
        p = c_off + _bmm(c_off.astype(BF16), y.astype(BF16))
        y = (y - p) - _bmm(y.astype(BF16), p.astype(BF16))

    egc = jnp.exp(gcc_b)
    rhs = jnp.concatenate([vb, kb * egc], axis=-1)
    uw = rhs + _bmm(y.astype(BF16), rhs.astype(BF16))
    u4 = uw[:, :, :dh].reshape(DN_HEADS, ns, PAIR, dh)
    w4 = uw[:, :, dh:].astype(BF16).reshape(DN_HEADS, ns, PAIR, dh)
    qg4 = (qf * egc).astype(BF16).reshape(DN_HEADS, ns, PAIR, dh)
    kg_t4 = jnp.swapaxes(kf * jnp.exp(rem_b), 1, 2).reshape(DN_HEADS, ns, dh, PAIR)
    lane_chunk = [(c_chunk == c)[None] for c in range(2)]
    zeros_half = jnp.zeros((DN_HEADS, DN_CHUNK, dh), F32)

    st = state_ref[...]
    vn_subs, oi_subs = [], []
    for s in range(ns):
        vn_c, oi_c = [], []
        for c in range(2):
            rows = slice(c * DN_CHUNK, (c + 1) * DN_CHUNK)
            st_b = st.astype(BF16)
            vn = u4[:, s, rows] - _bmm(w4[:, s, rows], st_b)
            oi_c.append(_bmm(qg4[:, s, rows], st_b))
            vn_pad = jnp.concatenate([vn, zeros_half] if c == 0 else [zeros_half, vn], axis=1)
            kg_c = jnp.where(lane_chunk[c], kg_t4[:, s], 0.0).astype(BF16)
            st = st * jnp.exp(tot_b[s][c]) + _bmm(kg_c, vn_pad.astype(BF16))
            vn_c.append(vn)
        vn_subs.append(jnp.concatenate(vn_c, axis=1))
        oi_subs.append(jnp.concatenate(oi_c, axis=1))
    state_ref[...] = st

    vn_all = jnp.stack(vn_subs, axis=1).reshape(gn, PAIR, dh)
    o_inter = jnp.stack(oi_subs, axis=1).reshape(gn, PAIR, dh)
    o = o_inter + _bmm(qk.astype(BF16), vn_all.astype(BF16))
    ms = jnp.mean(o * o, axis=-1, keepdims=True)
    on = (o * lax.rsqrt(ms + NORM_EPS)) * nrm_ref[...]
    for h in range(DN_HEADS):
        for s in range(ns):
            rows = slice(s * PAIR, (s + 1) * PAIR)
            hs = slice(h * dh, (h + 1) * dh)
            zh = z_ref[rows, hs].astype(F32)
            o_ref[rows, hs] = (on[h * ns + s] * (zh * jax.nn.sigmoid(zh))).astype(BF16)


def _delta_call(q, k, v, small, z, prm, nrm, batch, seq):
    t = batch * seq
    tb = min(DELTA_TB, seq)
    nt = seq // tb

    def row_spec(n):
        return pl.BlockSpec((tb, n), lambda b, i: (b * nt + i, 0))

    head_spec = pl.BlockSpec((None, DN_HEADS, tb, DN_HEAD_DIM), lambda b, i: (b, 0, i, 0))
    kern = functools.partial(_delta_kernel, tb=tb)
    return pl.pallas_call(
        kern,
        grid=(batch, nt),
        in_specs=[
            head_spec, head_spec, head_spec, row_spec(LANES),
            row_spec(DN_WIDTH),
            pl.BlockSpec((8, LANES), lambda b, i: (0, 0)),
            pl.BlockSpec((1, DN_HEAD_DIM), lambda b, i: (0, 0)),
        ],
        out_specs=row_spec(DN_WIDTH),
        out_shape=jax.ShapeDtypeStruct((t, DN_WIDTH), BF16),
        scratch_shapes=[pltpu.VMEM((DN_HEADS, DN_HEAD_DIM, DN_HEAD_DIM), F32)],
        compiler_params=_cparams(("parallel", "arbitrary")),
        name="delta_rule",
    )(q, k, v, small, z, prm, nrm)


def _attn_kernel(q_ref, kp_ref, kc_ref, vp_ref, vc_ref, bias_ref, o_ref, lse_ref, *, planes):
    sub = DA_BLOCK // planes

    def load(ref):
        if planes == 1:
            return ref[...]
        return jnp.stack([jnp.concatenate([ref[a, p] for a in range(planes)], axis=0)
                          for p in range(DA_PAIRS)], axis=0)

    lane = lax.broadcasted_iota(jnp.int32, (DA_PAIRS, DA_BLOCK, LANES), 2)
    low = lane < DA_HEAD_DIM
    q = load(q_ref) * (DA_HEAD_DIM ** -0.5)
    zero = jnp.zeros_like(q)
    qq = jnp.concatenate([jnp.where(low, q, zero), jnp.where(low, zero, q)], axis=1)
    k2 = jnp.concatenate([load(kp_ref), load(kc_ref)], axis=1)
    v2 = jnp.concatenate([load(vp_ref), load(vc_ref)], axis=1)
    v2 = jnp.concatenate([v2, jnp.ones_like(v2)], axis=2)

    s = _bmm_nt(qq, k2) + bias_ref[...]
    mx = jnp.max(s, axis=-1, keepdims=True)
    p = jnp.exp(s - mx).astype(BF16)
    acc = _bmm(p, v2)
    den = acc[:, :, LANES:]
    o = acc[:, :, :LANES] / den
    o = jnp.where(low, o[:, :DA_BLOCK], o[:, DA_BLOCK:]).astype(BF16)
    if planes == 1:
        o_ref[...] = o
    else:
        for a in range(planes):
            for p in range(DA_PAIRS):
                o_ref[a, p] = o[p, a * sub:(a + 1) * sub]

    lse = mx + jnp.log(den)
    lane2 = lax.broadcasted_iota(jnp.int32, (DA_BLOCK, LANES), 1)
    lse_tile = jnp.zeros((DA_BLOCK, LANES), F32)
    for h in range(DA_HEADS):
        rows = slice((h % 2) * DA_BLOCK, (h % 2 + 1) * DA_BLOCK)
        lse_tile = jnp.where(lane2 == h, lse[h // 2, rows], lse_tile)
    if planes == 1:
        lse_ref[...] = lse_tile
    else:
        for a in range(planes):
            lse_ref[a] = lse_tile[a * sub:(a + 1) * sub]


def _attn_bias_tables(dilation, planes):
    sub = DA_BLOCK // planes
    pos = (np.arange(DA_BLOCK) % sub) * planes + np.arange(DA_BLOCK) // sub
    qi = pos[:, None]
    ki = pos[None, :]
    dist = np.concatenate([qi - ki + DA_BLOCK, qi - ki], axis=1)
    valid_other = np.concatenate([ki >= qi, ki <= qi], axis=1)
    valid_first = np.concatenate([np.zeros((DA_BLOCK, DA_BLOCK), bool), ki <= qi], axis=1)
    slopes = np.float32(2.0) ** (-np.float32(ALIBI_MAX_EXP) * np.arange(1, DA_HEADS + 1, dtype=np.float32)
                                 / np.float32(DA_HEADS))
    pen = -slopes[:, None, None] * (dist * dilation).astype(np.float32)[None]
    tables = []
    for valid in (valid_first, valid_other):
        t = np.where(valid[None], pen, np.float32(NEG_BIG)).astype(np.float32)
        tables.append(t.reshape(DA_PAIRS, 2 * DA_BLOCK, 2 * DA_BLOCK))
    return jnp.asarray(np.stack(tables, axis=0))


def _attn_call(q, k, v, batch, seq, dilation):
    r = dilation
    nb = seq // r // DA_BLOCK
    if r == 1:
        planes = 1
        grid = (batch, nb)
        blk = (None, DA_PAIRS, DA_BLOCK, LANES)
        cur = pl.BlockSpec(blk, lambda b, i: (b, 0, i, 0))
        prev = pl.BlockSpec(blk, lambda b, i: (b, 0, jnp.maximum(i - 1, 0), 0))
        lse_spec = pl.BlockSpec((DA_BLOCK, LANES), lambda b, i: (b * nb + i, 0))
        bias_spec = pl.BlockSpec((None, DA_PAIRS, 2 * DA_BLOCK, 2 * DA_BLOCK),
                                 lambda b, i: (jnp.minimum(i, 1), 0, 0, 0))
        o_shape = jax.ShapeDtypeStruct((batch, DA_PAIRS, seq, LANES), BF16)
        lse_shape = jax.ShapeDtypeStruct((batch * seq, LANES), F32)
        sem = ("parallel", "arbitrary")
    elif r == DR_MOD:
        planes = 1
        grid = (batch, DR_SUB, DR_SUB, nb)
        blk = (None, None, None, DA_PAIRS, DA_BLOCK, LANES)
        cur = pl.BlockSpec(blk, lambda b, a, c, i: (b, a, c, 0, i, 0))
        prev = pl.BlockSpec(blk, lambda b, a, c, i: (b, a, c, 0, jnp.maximum(i - 1, 0), 0))
        lse_spec = pl.BlockSpec((None, None, None, DA_BLOCK, LANES), lambda b, a, c, i: (b, a, c, i, 0))
        bias_spec = pl.BlockSpec((None, DA_PAIRS, 2 * DA_BLOCK, 2 * DA_BLOCK),
                                 lambda b, a, c, i: (jnp.minimum(i, 1), 0, 0, 0))
        o_shape = _residue_shape(batch, seq, DA_PAIRS, BF16)
        lse_shape = _residue_shape(batch, seq, 0, F32)
        sem = ("parallel", "parallel", "parallel", "arbitrary")
    else:
        assert r == DR_SUB
        planes = DR_SUB
        sub = DA_BLOCK // planes
        grid = (batch, DR_SUB, nb)
        blk = (None, None, planes, DA_PAIRS, sub, LANES)
        cur = pl.BlockSpec(blk, lambda b, a, i: (b, a, 0, 0, i, 0))
        prev = pl.BlockSpec(blk, lambda b, a, i: (b, a, 0, 0, jnp.maximum(i - 1, 0), 0))
        lse_spec = pl.BlockSpec((None, None, planes, sub, LANES), lambda b, a, i: (b, a, 0, i, 0))
        bias_spec = pl.BlockSpec((None, DA_PAIRS, 2 * DA_BLOCK, 2 * DA_BLOCK),
                                 lambda b, a, i: (jnp.minimum(i, 1), 0, 0, 0))
        o_shape = _residue_shape(batch, seq, DA_PAIRS, BF16)
        lse_shape = _residue_shape(batch, seq, 0, F32)
        sem = ("parallel", "parallel", "arbitrary")

    return pl.pallas_call(
        functools.partial(_attn_kernel, planes=planes),
        grid=grid,
        in_specs=[cur, prev, cur, prev, cur, bias_spec],
        out_specs=[cur, lse_spec],
        out_shape=[o_shape, lse_shape],
        compiler_params=_cparams(sem),
        name=f"dilated_attn_r{r}",
    )(q, k, k, v, v, _attn_bias_tables(r, planes))


def _merge_kernel(h_ref, gt_ref, oa_ref, o1_ref, o2_ref, o3_ref, l1_ref, l2_ref, l3_ref,
                  ga_ref, gb_ref, wa_ref, wb_ref, wo_ref, ex_ref, out_ref, nat_ref, *, tm):
    rows = tm // DR_MOD
    ex = ex_ref[...]
    l2 = l2_ref[...].reshape(tm, LANES)
    l3 = l3_ref[...].reshape(tm, LANES)
    m23 = jnp.maximum(l2, l3)
    e2 = jnp.exp(l2 - m23)
    e3 = jnp.exp(l3 - m23)
    e2x = _mm_split2_rhs(e2, ex)
    e3x = _mm_split2_rhs(e3, ex)
    slabs = []
    for p in range(DA_PAIRS):
        ls = slice(p * LANES, (p + 1) * LANES)
        slabs.append(e2x[:, ls] * o2_ref[:, :, p].reshape(tm, LANES).astype(F32)
                     + e3x[:, ls] * o3_ref[:, :, p].reshape(tm, LANES).astype(F32))
    slabs += [m23, e2 + e3]
    for s, val in enumerate(slabs):
        for p16 in range(DR_MOD):
            g = (p16 % DR_SUB) * DR_SUB + p16 // DR_SUB
            nat_ref[s, pl.ds(p16, rows, stride=DR_MOD), :] = val[g * rows:(g + 1) * rows]

    l1 = l1_ref[...]
    m23n = nat_ref[DA_PAIRS]
    mx = jnp.maximum(l1, m23n)
    e1 = jnp.exp(l1 - mx)
    e23 = jnp.exp(m23n - mx)
    den = e1 + e23 * nat_ref[DA_PAIRS + 1]
    w1 = _mm_split2_rhs(e1 / den, ex)
    w23 = _mm_split2_rhs(e23 / den, ex)
    ob = []
    for p in range(DA_PAIRS):
        ls = slice(p * LANES, (p + 1) * LANES)
        ob.append((w1[:, ls] * o1_ref[p].astype(F32) + w23[:, ls] * nat_ref[p]).astype(BF16))
    yb = _dot(jnp.concatenate(ob, axis=-1), wb_ref[...])
    ya = _dot(oa_ref[...], wa_ref[...])
    merged = (jax.nn.sigmoid(ga_ref[...].astype(F32)) * ya
              + jax.nn.sigmoid(gb_ref[...].astype(F32)) * yb)
    mo = _dot(merged.astype(BF16), wo_ref[...])
    out_ref[...] = h_ref[...] + gt_ref[...] * mo


def _merge_call(h, mod, oa, outs, lses, ga, gb, wa, wb, wo, expand, layer, seq):
    t, d = h.shape
    tm = min(MERGE_TM, seq)
    tiles_per_batch = seq // tm

    def row_spec(n):
        return pl.BlockSpec((tm, n), lambda i: (i, 0))

    pair_spec = pl.BlockSpec((None, DA_PAIRS, tm, LANES),
                             lambda i: (i // tiles_per_batch, 0, i % tiles_per_batch, 0))
    def weight_spec(rows):
        return pl.BlockSpec((None, rows, d), lambda i: (layer, 0, 0), pipeline_mode=pl.Buffered(1))

    res_o = _residue_block(tm, tiles_per_batch, DA_PAIRS)
    res_l = _residue_block(tm, tiles_per_batch, 0)
    return pl.pallas_call(
        functools.partial(_merge_kernel, tm=tm),
        grid=(t // tm,),
        in_specs=[
            row_spec(d),
            pl.BlockSpec((None, None, None, 1, d), lambda i: (layer, 5, i // tiles_per_batch, 0, 0)),
            row_spec(DN_WIDTH),
            pair_spec, res_o, res_o,
            row_spec(LANES), res_l, res_l,
            row_spec(d), row_spec(d),
            weight_spec(DN_WIDTH), weight_spec(DA_WIDTH), weight_spec(d),
            pl.BlockSpec((LANES, DA_WIDTH), lambda i: (0, 0)),
        ],
        out_specs=row_spec(d),
        out_shape=jax.ShapeDtypeStruct((t, d), F32),
        scratch_shapes=[pltpu.VMEM((DA_PAIRS + 2, tm, LANES), F32)],
        compiler_params=_cparams(("parallel",)),
        name="merge_out",
    )(h, mod, oa, outs[0], outs[1], outs[2], lses[0], lses[1], lses[2], ga, gb, wa, wb, wo, expand)


def _reorder_w_in(w_in):
    o_small = 4 * DN_WIDTH
    main_a = w_in[:, :, :o_small]
    small = w_in[:, :, o_small:o_small + 2 * DN_HEADS]
    main_b = w_in[:, :, o_small + 2 * DN_HEADS:]
    pad = jnp.zeros(w_in.shape[:2] + (LANES - 2 * DN_HEADS,), w_in.dtype)
    return jnp.concatenate([main_a, main_b, small, pad], axis=-1).astype(BF16)


def kernel(x, c, ada_w, ada_b, ln_ffn1, ln_mix, ln_ffn2, ffn1_wg, ffn1_wu, ffn1_wd, w_in, conv_w,
           a_log, dt_bias, dn_norm, w_a, w_b, w_o, ffn2_wg, ffn2_wu, ffn2_wd, final_norm):
    batch, seq, d = x.shape
    depth = ada_w.shape[0]
    t = batch * seq

    mod = _ada_call(c, ada_w, ada_b)
    mod = mod.reshape(depth, batch, N_ADA, 1, d).transpose(0, 2, 1, 3, 4)

    w_r = _reorder_w_in(w_in)
    bf = lambda a: a.astype(BF16)
    f1g, f1u, f1d = bf(ffn1_wg), bf(ffn1_wu), bf(ffn1_wd)
    f2g, f2u, f2d = bf(ffn2_wg), bf(ffn2_wu), bf(ffn2_wd)
    wa, wb, wo = bf(w_a), bf(w_b), bf(w_o)
    ln1 = ln_ffn1.reshape(depth, 1, d)
    lnm = ln_mix.reshape(depth, 1, d)
    ln2 = ln_ffn2.reshape(depth, 1, d)

    lane = np.arange(LANES)
    expand = jnp.asarray((lane[:, None] == (np.arange(DA_WIDTH)[None, :] // DA_HEAD_DIM)), BF16)
    zpad = jnp.zeros((depth, DN_HEADS), F32)
    prm_rows = jnp.stack([jnp.concatenate([zpad, a_log.astype(F32)], axis=-1),
                          jnp.concatenate([zpad, dt_bias.astype(F32)], axis=-1)], axis=1)
    prm = jnp.pad(prm_rows, ((0, 0), (0, 6), (0, LANES - 2 * DN_HEADS)))

    h = x.reshape(t, d)
    for l in range(depth):
        h = _ffn_call(h, mod, (0, 1, 2), ln1, f1g, f1u, f1d, l, seq)
        (q, k, v, z, daq, dak, dav, ga, gb, small,
         daq_res, dak_res, dav_res) = _inproj_call(h, mod, lnm, w_r, conv_w, l, seq)
        oa = _delta_call(q, k, v, small, z, prm[l], dn_norm[l].reshape(1, DN_HEAD_DIM), batch, seq)
        outs, lses = [], []
        for _, dilation in DA_PATTERNS:
            if dilation == 1:
                o_p, lse_p = _attn_call(daq, dak, dav, batch, seq, dilation)
            else:
                o_p, lse_p = _attn_call(daq_res, dak_res, dav_res, batch, seq, dilation)
            outs.append(o_p)
            lses.append(lse_p)
        h = _merge_call(h, mod, oa, outs, lses, ga, gb, wa, wb, wo, expand, l, seq)
        h = _ffn_call(h, mod, (6, 7, 8), ln2, f2g, f2u, f2d, l, seq,
                      final_norm_w=final_norm if l == depth - 1 else None)
    return h.reshape(batch, seq, d)
```
